```python
import jax, jax.numpy as jnp
from jax import lax
import numpy as np

D_MODEL = 2048
BATCH = 32
SEQ = 256
DEPTH = 2
DEC_BATCH = 2
DEC_SEQ = 2048
PAST_LEN = 512

GRID_W = 64
SSM_EXPAND = 2
D_INNER = SSM_EXPAND * D_MODEL
SSM_HEAD_DIM = 64
N_HEADS = D_INNER // SSM_HEAD_DIM
N_GROUPS = 8
D_STATE = 128
CHUNK = 128
CONV_W = 3
CONV_DIM = D_INNER + 2 * N_GROUPS * D_STATE
NORM_GROUPS = N_GROUPS
D_SC = D_MODEL
SC_CONV_W = 3
D_FF = 5632
EPS = 1e-6
IN_COLS = D_INNER + CONV_DIM + 2 * N_HEADS + 3 * D_SC + 2 * D_MODEL

kernel_name = "hybrid_ssd_shortconv_dit_prefix_step"


def _cuts(sizes):
    out, s = [], 0
    for n in sizes[:-1]:
        s += n
        out.append(s)
    return out


def rmsnorm(x, w):
    xf = x.astype(jnp.float32)
    y = xf * lax.rsqrt(jnp.mean(xf * xf, axis=-1, keepdims=True) + EPS)
    return (y * w.astype(jnp.float32)).astype(x.dtype)


def gated_group_rmsnorm(y, z, w):
    yf = (y * jax.nn.silu(z)).astype(jnp.float32)
    b, l, d = yf.shape
    yg = yf.reshape(b, l, NORM_GROUPS, d // NORM_GROUPS)
    yg = yg * lax.rsqrt(jnp.mean(yg * yg, axis=-1, keepdims=True) + EPS)
    return (yg.reshape(b, l, d) * w.astype(jnp.float32)).astype(y.dtype)


def conv_rows(x, w, rows, row_len):
    b, l, ch = x.shape
    xp = jnp.pad(x.reshape(b, rows, row_len, ch), ((0, 0), (0, 0), (1, 1), (0, 0)))
    y = xp[:, :, 0:row_len] * w[0] + xp[:, :, 1:row_len + 1] * w[1] + xp[:, :, 2:row_len + 2] * w[2]
    return y.reshape(b, l, ch)


def conv_grid(x, w, rows, row_len):
    b, l, ch = x.shape
    xp = jnp.pad(x.reshape(b, rows, row_len, ch), ((0, 0), (1, 1), (1, 1), (0, 0)))
    y = sum(xp[:, i:i + rows, j:j + row_len] * w[i, j] for i in range(3) for j in range(3))
    return y.reshape(b, l, ch)


def ssd_scan(x, dt, a, bm, cm, h0):
    f32 = jnp.float32
    bsz, seq, nh, hd = x.shape
    ng, ns = bm.shape[2], bm.shape[3]
    rep = nh // ng
    nc = seq // CHUNK
    x = x.astype(f32).reshape(bsz, nc, CHUNK, ng, rep, hd)
    dt = dt.astype(f32).reshape(bsz, nc, CHUNK, ng, rep)
    bm = bm.astype(f32).reshape(bsz, nc, CHUNK, ng, ns)
    cm = cm.astype(f32).reshape(bsz, nc, CHUNK, ng, ns)
    acs = jnp.cumsum(dt * a.astype(f32).reshape(ng, rep), axis=2)
    lower = jnp.tril(jnp.ones((CHUNK, CHUNK), bool))[:, :, None, None]
    seg = acs[:, :, :, None] - acs[:, :, None, :]
    decay_qk = jnp.exp(jnp.where(lower, seg, -jnp.inf))
    cb = jnp.einsum('bcqgn,bckgn->bcqkg', cm, bm)
    y_diag = jnp.einsum('bcqkg,bcqkgr,bckgr,bckgrp->bcqgrp', cb, decay_qk, dt, x)
    decay_to_end = jnp.exp(acs[:, :, -1:] - acs)
    chunk_states = jnp.einsum('bckgn,bckgr,bckgrp->bcgrpn', bm, decay_to_end * dt, x)
    chunk_decay = jnp.exp(acs[:, :, -1])

    def step(h, inp):
        dec, st = inp
        return dec[..., None, None] * h + st, h

    h_init = h0.astype(f32).reshape(bsz, ng, rep, hd, ns)
    h_final, h_enter = lax.scan(step, h_init, (jnp.moveaxis(chunk_decay, 1, 0), jnp.moveaxis(chunk_states, 1, 0)))
    h_enter = jnp.moveaxis(h_enter, 0, 1)
    y_off = jnp.einsum('bcqgn,bcgrpn,bcqgr->bcqgrp', cm, h_enter, jnp.exp(acs))
    y = (y_diag + y_off).reshape(bsz, seq, nh, hd)
    return y, h_final.reshape(bsz, nh, hd, ns)


def trunk_layer(x, cvec, h0, rows, row_len, w_ada, b_ada, norm_w, w_in, conv_w, conv_b, a_log, dt_bias,
                d_skip, ssm_norm_w, w_out_ssm, sc_conv_w, w_out_sc, w_o, w_up, ffn_conv_w, w_down):
    bsz, seq = x.shape[0], x.shape[1]
    mod = jax.nn.silu(cvec) @ w_ada + b_ada
    sh1, sc1, g1, sh2, sc2, g2 = jnp.split(mod[:, None, :], 6, axis=-1)

    h = rmsnorm(x, norm_w[0]) * (1 + sc1) + sh1
    proj = h @ w_in
    z, xbc, dt, sc_b, sc_c, sc_h, gates = jnp.split(
        proj, _cuts([D_INNER, CONV_DIM, 2 * N_HEADS, D_SC, D_SC, D_SC, 2 * D_MODEL]), axis=-1)

    xbc = jax.nn.silu(conv_rows(xbc, conv_w, rows, row_len) + conv_b)
    xs, bm, cm = jnp.split(xbc, [D_INNER, D_INNER + N_GROUPS * D_STATE], axis=-1)
    xs_h = xs.reshape(bsz, seq, N_HEADS, SSM_HEAD_DIM)
    bm = bm.reshape(bsz, seq, N_GROUPS, D_STATE)
    cm = cm.reshape(bsz, seq, N_GROUPS, D_STATE)
    dt = jax.nn.softplus(dt.reshape(bsz, seq, 2, N_HEADS) + dt_bias)
    a = -jnp.exp(a_log.astype(jnp.float32))
    y_f, h_f = ssd_scan(xs_h, dt[:, :, 0], a[0], bm, cm, h0[:, 0])
    y_b, h_b = ssd_scan(jnp.flip(xs_h, 1), jnp.flip(dt[:, :, 1], 1), a[1],
                        jnp.flip(bm, 1), jnp.flip(cm, 1), h0[:, 1])
    y = (y_f + jnp.flip(y_b, 1) + d_skip[:, None] * xs_h).astype(x.dtype)
    y = gated_group_rmsnorm(y.reshape(bsz, seq, D_INNER), z, ssm_norm_w)
    y_ssm = y @ w_out_ssm

    y_sc = (sc_b * conv_rows(sc_c * sc_h, sc_conv_w, rows, row_len)) @ w_out_sc

    g_ssm, g_sc = jnp.split(jax.nn.sigmoid(gates), 2, axis=-1)
    x = x + g1 * ((g_ssm * y_ssm + g_sc * y_sc) @ w_o)

    h = rmsnorm(x, norm_w[1]) * (1 + sc2) + sh2
    u = conv_grid(h @ w_up, ffn_conv_w, rows, row_len)
    u_gate, u_val = jnp.split(u, 2, axis=-1)
    x = x + g2 * ((jax.nn.silu(u_gate) * u_val) @ w_down)
    return x, jnp.stack([h_f, h_b], axis=1)


def setup_inputs(seed: int = 0) -> dict:
    key = jax.random.key(seed)
    ks = jax.random.split(key, 24)
    f32 = jnp.float32

    def nrm(k, shape, scale):
        return jax.random.normal(k, shape, f32) * scale

    dt_init = jnp.exp(jax.random.uniform(ks[12], (DEPTH, 2, N_HEADS), f32, float(np.log(1e-3)), float(np.log(1e-1))))
    return {
        "x_prompt": nrm(ks[0], (BATCH, SEQ, D_MODEL), 1.0),
        "x_sample": nrm(ks[1], (DEC_BATCH, DEC_SEQ, D_MODEL), 1.0),
        "state_ssm": nrm(ks[2], (DEC_BATCH, DEPTH, 2, N_HEADS, SSM_HEAD_DIM, D_STATE), 0.5),
        "c": nrm(ks[3], (DEC_BATCH, D_MODEL), 1.0),
        "c_ctx": nrm(ks[4], (D_MODEL,), 1.0),
        "w_ada": nrm(ks[5], (DEPTH, D_MODEL, 6 * D_MODEL), 0.5 * D_MODEL ** -0.5),
        "b_ada": nrm(ks[6], (DEPTH, 6 * D_MODEL), 0.02),
        "norm_w": 1.0 + nrm(ks[7], (DEPTH, 2, D_MODEL), 0.02),
        "w_in": nrm(ks[8], (DEPTH, D_MODEL, IN_COLS), D_MODEL ** -0.5),
        "conv_w": nrm(ks[9], (DEPTH, CONV_W, CONV_DIM), CONV_W ** -0.5),
        "conv_b": nrm(ks[10], (DEPTH, CONV_DIM), 0.02),
        "a_log": jnp.log(jax.random.uniform(ks[11], (DEPTH, 2, N_HEADS), f32, 1.0, 16.0)),
        "dt_bias": dt_init + jnp.log(-jnp.expm1(-dt_init)),
        "d_skip": 1.0 + nrm(ks[13], (DEPTH, N_HEADS), 0.1),
        "ssm_norm_w": 1.0 + nrm(ks[14], (DEPTH, D_INNER), 0.02),
        "w_out_ssm": nrm(ks[15], (DEPTH, D_INNER, D_MODEL), D_INNER ** -0.5),
        "sc_conv_w": nrm(ks[16], (DEPTH, SC_CONV_W, D_SC), SC_CONV_W ** -0.5),
        "w_out_sc": nrm(ks[17], (DEPTH, D_SC, D_MODEL), D_SC ** -0.5),
        "w_o": nrm(ks[18], (DEPTH, D_MODEL, D_MODEL), D_MODEL ** -0.5),
        "w_up": nrm(ks[19], (DEPTH, D_MODEL, 2 * D_FF), D_MODEL ** -0.5),
        "ffn_conv_w": nrm(ks[20], (DEPTH, 3, 3, 2 * D_FF), 1.0 / 3.0),
        "w_down": nrm(ks[21], (DEPTH, D_FF, D_MODEL), D_FF ** -0.5),
        "final_norm_w": 1.0 + nrm(ks[22], (D_MODEL,), 0.02),
    }


def reference(x_prompt, x_sample, state_ssm, c, c_ctx, w_ada, b_ada, norm_w, w_in, conv_w, conv_b, a_log,
              dt_bias, d_skip, ssm_norm_w, w_out_ssm, sc_conv_w, w_out_sc, w_o, w_up, ffn_conv_w, w_down,
              final_norm_w):
    ctx_len = x_prompt.shape[1]
    lat_rows = x_sample.shape[1] // GRID_W
    zero_state = jnp.zeros((x_prompt.shape[0], 2, N_HEADS, SSM_HEAD_DIM, D_STATE), jnp.float32)
    cvec_ctx = c_ctx[None, :]
    h_ctx = x_prompt
    h_lat = x_sample
    ctx_states = []
    for l in range(DEPTH):
        lw = (w_ada[l], b_ada[l], norm_w[l], w_in[l], conv_w[l], conv_b[l], a_log[l], dt_bias[l], d_skip[l],
              ssm_norm_w[l], w_out_ssm[l], sc_conv_w[l], w_out_sc[l], w_o[l], w_up[l], ffn_conv_w[l], w_down[l])
        h_ctx, st = trunk_layer(h_ctx, cvec_ctx, zero_state, 1, ctx_len, *lw)
        ctx_states.append(st)
        h_lat, _ = trunk_layer(h_lat, c, state_ssm[:, l], lat_rows, GRID_W, *lw)
    y_prompt = rmsnorm(h_ctx, final_norm_w)
    y_sample = rmsnorm(h_lat, final_norm_w)
    new_state_ssm = jnp.stack(ctx_states, axis=1)
    return (y_prompt, y_sample, new_state_ssm)
```

```python
import functools

import jax
import jax.numpy as jnp
from jax import lax
from jax.experimental import pallas as pl
from jax.experimental.pallas import tpu as pltpu

F32 = jnp.float32
BF16 = jnp.bfloat16

D_MODEL = 2048
BATCH = 32
SEQ = 256
DEPTH = 2
DEC_BATCH = 2
DEC_SEQ = 2048
GRID_W = 64
D_INNER = 4096
HEAD_DIM = 64
N_HEADS = 64
N_GROUPS = 8
HEADS_PER_GROUP = N_HEADS // N_GROUPS
GROUP_DIM = D_INNER // N_GROUPS
D_STATE = 128
CHUNK = 128
CONV_DIM = D_INNER + 2 * N_GROUPS * D_STATE
D_FF = 5632
EPS = 1e-6
N_CTX_TOK = BATCH * SEQ
N_LAT_TOK = DEC_BATCH * DEC_SEQ
N_TOK = N_CTX_TOK + N_LAT_TOK
COL_DT = D_INNER + CONV_DIM
COL_P2 = COL_DT + 2 * N_HEADS
P2_COLS = 3 * D_MODEL + 2 * D_MODEL
MOD_ROWS = 16
SSD_BLOCK = 2048
CONV_PIECE = 256
LANES = 128
VMEM_LIMIT = 60 * 1024 * 1024


def _sigmoid(v):
    return 1.0 / (1.0 + jnp.exp(-v))


def _silu(v):
    return v * _sigmoid(v)


def _softplus(v):
    return jnp.maximum(v, 0.0) + jnp.log1p(jnp.exp(-jnp.abs(v)))


def _group_of_tile(i, tm):
    n_ctx = N_CTX_TOK // tm
    per_lat = DEC_SEQ // tm
    return jnp.where(i < n_ctx, 0, 1 + lax.div(jnp.maximum(i - n_ctx, 0), per_lat))


def _params(*sem):
    return pltpu.CompilerParams(dimension_semantics=sem, vmem_limit_bytes=VMEM_LIMIT)


def _mod_kernel(c_ref, w_ref, b_ref, o_ref):
    a = _silu(c_ref[...]).astype(BF16)
    o_ref[...] = jnp.dot(a, w_ref[...].astype(BF16), preferred_element_type=F32) + b_ref[...]


def _mod_call(cvec, w_ada, b_ada):
    tn = 1024
    n = 6 * D_MODEL
    return pl.pallas_call(
        _mod_kernel,
        grid=(DEPTH, n // tn),
        in_specs=[
            pl.BlockSpec((MOD_ROWS, D_MODEL), lambda l, j: (0, 0)),
            pl.BlockSpec((None, D_MODEL, tn), lambda l, j: (l, 0, j)),
            pl.BlockSpec((None, 1, tn), lambda l, j: (l, 0, j)),
        ],
        out_specs=pl.BlockSpec((None, MOD_ROWS, tn), lambda l, j: (l, 0, j)),
        out_shape=jax.ShapeDtypeStruct((DEPTH, MOD_ROWS, n), F32),
        compiler_params=_params("arbitrary", "arbitrary"),
        name="adaln_mod",
    )(cvec, w_ada, b_ada.reshape(DEPTH, 1, n))


def _prenorm_kernel(x_ref, nw_ref, sh_ref, sc_ref, o_ref, *, tm):
    g = _group_of_tile(pl.program_id(0), tm)
    x = x_ref[...]
    ms = jnp.mean(x * x, axis=-1, keepdims=True)
    y = x * lax.rsqrt(ms + EPS) * nw_ref[...]
    o_ref[...] = (y * (1.0 + sc_ref[pl.ds(g, 1), :]) + sh_ref[pl.ds(g, 1), :]).astype(BF16)


def _prenorm_call(x_all, norm_w, mod, l, which):
    tm = 512
    return pl.pallas_call(
        functools.partial(_prenorm_kernel, tm=tm),
        grid=(N_TOK // tm,),
        in_specs=[
            pl.BlockSpec((tm, D_MODEL), lambda i: (i, 0)),
            pl.BlockSpec((None, 1, D_MODEL), lambda i: (2 * l + which, 0, 0)),
            pl.BlockSpec((None, MOD_ROWS, D_MODEL), lambda i: (l, 0, 3 * which)),
            pl.BlockSpec((None, MOD_ROWS, D_MODEL), lambda i: (l, 0, 3 * which + 1)),
        ],
        out_specs=pl.BlockSpec((tm, D_MODEL), lambda i: (i, 0)),
        out_shape=jax.ShapeDtypeStruct((N_TOK, D_MODEL), BF16),
        compiler_params=_params("arbitrary"),
        name=f"prenorm{which}",
    )(x_all, norm_w.reshape(DEPTH * 2, 1, D_MODEL), mod, mod)


def _mm_kernel(*refs, n_extra, epilogue):
    h_ref, w_ref = refs[0], refs[1]
    extra = refs[2:2 + n_extra]
    o_ref, wb_ref = refs[2 + n_extra], refs[3 + n_extra]
    i = pl.program_id(1)

    @pl.when(i == 0)
    def _():
        w = w_ref[0] if len(w_ref.shape) == 3 else w_ref[...]
        wb_ref[...] = w.astype(BF16)

    acc = jnp.dot(h_ref[...], wb_ref[...], preferred_element_type=F32)
    o_ref[...] = epilogue(acc, i, *extra).astype(o_ref.dtype)


def _ws_matmul(h, w, l, col_off, n_cols, tm, tn, out_dtype, name, extra=(), extra_specs=(), epilogue=None):
    m, k = h.shape
    if epilogue is None:
        epilogue = lambda acc, i: acc
    if col_off % tn == 0:
        w_spec = pl.BlockSpec((None, k, tn), lambda j, i: (l, 0, col_off // tn + j))
    else:
        w_spec = pl.BlockSpec(
            (pl.Element(1), pl.Element(k), pl.Element(tn)),
            lambda j, i: (l, 0, pl.multiple_of(col_off + j * tn, LANES)))
    return pl.pallas_call(
        functools.partial(_mm_kernel, n_extra=len(extra), epilogue=epilogue),
        grid=(n_cols // tn, m // tm),
        in_specs=[pl.BlockSpec((tm, k), lambda j, i: (i, 0)), w_spec, *extra_specs],
        out_specs=pl.BlockSpec((tm, tn), lambda j, i: (i, j)),
        out_shape=jax.ShapeDtypeStruct((m, n_cols), out_dtype),
        scratch_shapes=[pltpu.VMEM((k, tn), BF16)],
        compiler_params=_params("arbitrary", "arbitrary"),
        name=name,
    )(h, w, *extra)


def _cumsum_rows(a, qi, reverse):
    s = 1
    while s < CHUNK:
        if reverse:
            a = a + jnp.where(qi < CHUNK - s, pltpu.roll(a, CHUNK - s, 0), 0.0)
        else:
            a = a + jnp.where(qi >= s, pltpu.roll(a, s, 0), 0.0)
        s *= 2
    return a


def _lane_bcast(a, j):
    return jnp.broadcast_to(a[:, j:j + 1], (a.shape[0], LANES))


def _ssd_kernel(z_ref, xs_ref, b_ref, c_ref, dt_ref, cwx_ref, cwb_ref, cwc_ref, cbx_ref, cbb_ref, cbc_ref,
                dtb_ref, al_ref, dsk_ref, nw_ref, h0_ref, y_ref, st_ref,
                xs_s, bt_s, cm_s, dtv_s, yacc_s, st_s):
    sb = pl.program_id(1)
    a_row = -jnp.exp(al_ref[...])
    qi = lax.broadcasted_iota(jnp.int32, (CHUNK, CHUNK), 0)
    ki = lax.broadcasted_iota(jnp.int32, (CHUNK, CHUNK), 1)
    lo = ki < HEAD_DIM
    lo_row = lo[0:1]

    def conv_silu(ref, w_ref, bias_ref, r0, row_len):
        x = ref[pl.ds(r0, CONV_PIECE), :]
        pos = lax.broadcasted_iota(jnp.int32, x.shape, 0) & (row_len - 1)
        prev = jnp.where(pos == 0, 0.0, pltpu.roll(x, 1, 0))
        nxt = jnp.where(pos == row_len - 1, 0.0, pltpu.roll(x, CONV_PIECE - 1, 0))
        w = w_ref[...]
        return _silu(prev * w[0:1] + x * w[1:2] + nxt * w[2:3] + bias_ref[...])

    def prepass(row_len):
        def piece(p, carry):
            r0 = pl.multiple_of(p * CONV_PIECE, CONV_PIECE)
            rows = pl.ds(r0, CONV_PIECE)
            xs_s[rows, :] = conv_silu(xs_ref, cwx_ref, cbx_ref, r0, row_len)
            bm = conv_silu(b_ref, cwb_ref, cbb_ref, r0, row_len)
            for cc in range(CONV_PIECE // CHUNK):
                bt_s[(CONV_PIECE // CHUNK) * p + cc] = bm[cc * CHUNK:(cc + 1) * CHUNK, :].T.astype(BF16)
            cm_s[rows, :] = conv_silu(c_ref, cwc_ref, cbc_ref, r0, row_len).astype(BF16)
            dtv_s[rows, :] = _softplus(dt_ref[rows, :] + dtb_ref[...])
            return carry
        lax.fori_loop(0, SSD_BLOCK // CONV_PIECE, piece, 0)

    def chunk(c, d):
        rows = pl.ds(pl.multiple_of(c * CHUNK, CHUNK), CHUNK)
        dtv = dtv_s[rows, :]
        cum = _cumsum_rows(dtv * a_row, qi, reverse=(d == 1))
        last = cum[CHUNK - 1:CHUNK, :] if d == 0 else cum[0:1, :]
        e_in = jnp.exp(cum)
        w_end = jnp.exp(last - cum) * dtv
        cum_t = cum.T
        dt_t = dtv.T
        cc_ = cm_s[rows, :]
        bt = bt_s[c]
        cb = jnp.dot(cc_, bt, preferred_element_type=F32)
        s_in = st_s[...]
        y_off = jnp.dot(cc_, s_in.astype(BF16), preferred_element_type=F32)
        xc = xs_s[rows, :]
        tri = (qi >= ki) if d == 0 else (qi <= ki)
        xw_tiles, dec_tiles, y_tiles = [], [], []
        for t in range(GROUP_DIM // LANES):
            tile = slice(t * LANES, (t + 1) * LANES)
            xt = xc[:, tile]
            ha, hb = HEADS_PER_GROUP * d + 2 * t, HEADS_PER_GROUP * d + 2 * t + 1
            ms = []
            for hl in (ha, hb):
                seg = _lane_bcast(cum, hl) - cum_t[hl:hl + 1, :]
                decay = jnp.exp(jnp.where(tri, seg, -jnp.inf))
                ms.append((cb * decay * dt_t[hl:hl + 1, :]).astype(BF16))
            lhs = jnp.concatenate(ms, axis=1)
            rhs = jnp.concatenate([jnp.where(lo, xt, 0.0), jnp.where(lo, 0.0, xt)], axis=0).astype(BF16)
            y_diag = jnp.dot(lhs, rhs, preferred_element_type=F32)
            e_b = jnp.where(lo, _lane_bcast(e_in, ha), _lane_bcast(e_in, hb))
            y_tiles.append(y_diag + y_off[:, tile] * e_b)
            w_b = jnp.where(lo, _lane_bcast(w_end, ha), _lane_bcast(w_end, hb))
            xw_tiles.append((xt * w_b).astype(BF16))
            dec_tiles.append(jnp.where(lo_row, _lane_bcast(last, ha), _lane_bcast(last, hb)))
        xw = jnp.concatenate(xw_tiles, axis=1)
        dec = jnp.exp(jnp.concatenate(dec_tiles, axis=1))
        st_s[...] = s_in * dec + jnp.dot(bt, xw, preferred_element_type=F32)
        if d == 0:
            yacc_s[rows, :] = jnp.concatenate(y_tiles, axis=1)
        else:
            y = yacc_s[rows, :] + jnp.concatenate(y_tiles, axis=1) + dsk_ref[...] * xc
            yz = y * _silu(z_ref[rows, :])
            msq = jnp.mean(yz * yz, axis=-1, keepdims=True)
            y_ref[rows, :] = (yz * lax.rsqrt(msq + EPS) * nw_ref[...]).astype(BF16)

    @pl.when(sb < N_CTX_TOK // SSD_BLOCK)
    def _context():
        prepass(SEQ)
        nc = SEQ // CHUNK

        def seq_body(s, carry):
            st_s[...] = jnp.zeros_like(st_s)
            for k in range(nc):
                chunk(s * nc + k, 0)
            st_ref[s, 0] = st_s[...].T
            st_s[...] = jnp.zeros_like(st_s)
            for k in range(nc - 1, -1, -1):
                chunk(s * nc + k, 1)
            st_ref[s, 1] = st_s[...].T
            return carry
        lax.fori_loop(0, SSD_BLOCK // SEQ, seq_body, 0)

    @pl.when(sb >= N_CTX_TOK // SSD_BLOCK)
    def _latent():
        prepass(GRID_W)
        nc = DEC_SEQ // CHUNK
        st_s[...] = h0_ref[0].T

        def fwd(c, carry):
            chunk(c, 0)
            return carry
        lax.fori_loop(0, nc, fwd, 0)
        st_s[...] = h0_ref[1].T

        def bwd(k, carry):
            chunk(nc - 1 - k, 1)
            return carry
        lax.fori_loop(0, nc, bwd, 0)


def _ssd_call(zx, dt_raw, conv_w, conv_b, dtb_x, alog_x, dsk, ssm_norm_w, state_ssm, l):
    n_ctx_blk = N_CTX_TOK // SSD_BLOCK
    n_blk = N_TOK // SSD_BLOCK
    seq_per_blk = SSD_BLOCK // SEQ
    xs_cb = D_INNER // GROUP_DIM
    b_cb = (2 * D_INNER) // D_STATE
    c_cb = b_cb + N_GROUPS
    wb_cb = D_INNER // D_STATE
    wc_cb = wb_cb + N_GROUPS
    rowvec = lambda width, cb: pl.BlockSpec((None, 1, width), lambda g, sb: (l, 0, cb + g))
    taps = lambda width, cb: pl.BlockSpec((None, 3, width), lambda g, sb: (l, 0, cb + g))
    h0 = state_ssm.reshape(DEC_BATCH, DEPTH, 2, N_GROUPS, GROUP_DIM, D_STATE)
    return pl.pallas_call(
        _ssd_kernel,
        grid=(N_GROUPS, n_blk),
        in_specs=[
            pl.BlockSpec((SSD_BLOCK, GROUP_DIM), lambda g, sb: (sb, g)),
            pl.BlockSpec((SSD_BLOCK, GROUP_DIM), lambda g, sb: (sb, xs_cb + g)),
            pl.BlockSpec((SSD_BLOCK, D_STATE), lambda g, sb: (sb, b_cb + g)),
            pl.BlockSpec((SSD_BLOCK, D_STATE), lambda g, sb: (sb, c_cb + g)),
            pl.BlockSpec((SSD_BLOCK, LANES), lambda g, sb: (sb, g)),
            taps(GROUP_DIM, 0), taps(D_STATE, wb_cb), taps(D_STATE, wc_cb),
            rowvec(GROUP_DIM, 0), rowvec(D_STATE, wb_cb), rowvec(D_STATE, wc_cb),
            rowvec(LANES, 0), rowvec(LANES, 0),
            rowvec(GROUP_DIM, 0), rowvec(GROUP_DIM, 0),
            pl.BlockSpec((None, None, 2, None, GROUP_DIM, D_STATE),
                         lambda g, sb: (jnp.maximum(sb - n_ctx_blk, 0), l, 0, g, 0, 0)),
        ],
        out_specs=[
            pl.BlockSpec((SSD_BLOCK, GROUP_DIM), lambda g, sb: (sb, g)),
            pl.BlockSpec((seq_per_blk, 2, None, GROUP_DIM, D_STATE),
                         lambda g, sb: (jnp.minimum(sb, n_ctx_blk - 1), 0, g, 0, 0)),
        ],
        out_shape=[
            jax.ShapeDtypeStruct((N_TOK, D_INNER), BF16),
            jax.ShapeDtypeStruct((BATCH, 2, N_GROUPS, GROUP_DIM, D_STATE), F32),
        ],
        scratch_shapes=[
            pltpu.VMEM((SSD_BLOCK, GROUP_DIM), F32),
            pltpu.VMEM((SSD_BLOCK // CHUNK, D_STATE, CHUNK), BF16),
            pltpu.VMEM((SSD_BLOCK, D_STATE), BF16),
            pltpu.VMEM((SSD_BLOCK, LANES), F32),
            pltpu.VMEM((SSD_BLOCK, GROUP_DIM), F32),
            pltpu.VMEM((D_STATE, GROUP_DIM), F32),
        ],
        compiler_params=_params("arbitrary", "arbitrary"),
        name="ssd",
    )(zx, zx, zx, zx, dt_raw, conv_w, conv_w, conv_w, conv_b, conv_b, conv_b,
      dtb_x, alog_x, dsk, ssm_norm_w, h0)


def _scgate_kernel(b_ref, c_ref, h_ref, w_ref, o_ref, *, tm):
    last = jnp.where(pl.program_id(0) < N_CTX_TOK // tm, SEQ - 1, GRID_W - 1)
    v = c_ref[...] * h_ref[...]
    pos = lax.broadcasted_iota(jnp.int32, v.shape, 0) & last
    prev = jnp.where(pos == 0, 0.0, pltpu.roll(v, 1, 0))
    nxt = jnp.where(pos == last, 0.0, pltpu.roll(v, tm - 1, 0))
    w = w_ref[...]
    o_ref[...] = (b_ref[...] * (prev * w[0:1] + v * w[1:2] + nxt * w[2:3])).astype(BF16)


def _scgate_call(p2, sc_conv_w, l):
    tm, tn = 512, 512
    nb = D_MODEL // tn
    return pl.pallas_call(
        functools.partial(_scgate_kernel, tm=tm),
        grid=(N_TOK // tm, nb),
        in_specs=[
            pl.BlockSpec((tm, tn), lambda i, j: (i, j)),
            pl.BlockSpec((tm, tn), lambda i, j: (i, nb + j)),
            pl.BlockSpec((tm, tn), lambda i, j: (i, 2 * nb + j)),
            pl.BlockSpec((None, 3, tn), lambda i, j: (l, 0, j)),
        ],
        out_specs=pl.BlockSpec((tm, tn), lambda i, j: (i, j)),
        out_shape=jax.ShapeDtypeStruct((N_TOK, D_MODEL), BF16),
        compiler_params=_params("arbitrary", "arbitrary"),
        name="scgate",
    )(p2, p2, p2, sc_conv_w)


def _ffnconv_kernel(g_ref, v_ref, wg_ref, wv_ref, o_ref):
    i = pl.program_id(0)
    n = g_ref.shape[0]
    t = lax.broadcasted_iota(jnp.int32, g_ref.shape, 0)

    def conv(x_ref, w_ref, row_len, vertical):
        x = x_ref[...]
        w = w_ref[...]
        pos = t & (row_len - 1)
        xl = jnp.where(pos == 0, 0.0, pltpu.roll(x, 1, 0))
        xr = jnp.where(pos == row_len - 1, 0.0, pltpu.roll(x, n - 1, 0))
        row = lambda a: xl * w[3 * a:3 * a + 1] + x * w[3 * a + 1:3 * a + 2] + xr * w[3 * a + 2:3 * a + 3]
        y = row(1)
        if vertical:
            y = y + jnp.where(t >= row_len, pltpu.roll(row(0), row_len, 0), 0.0)
            y = y + jnp.where(t < n - row_len, pltpu.roll(row(2), n - row_len, 0), 0.0)
        return y

    @pl.when(i < N_CTX_TOK // n)
    def _context():
        o_ref[...] = (_silu(conv(g_ref, wg_ref, SEQ, False)) * conv(v_ref, wv_ref, SEQ, False)).astype(BF16)

    @pl.when(i >= N_CTX_TOK // n)
    def _latent():
        o_ref[...] = (_silu(conv(g_ref, wg_ref, GRID_W, True)) * conv(v_ref, wv_ref, GRID_W, True)).astype(BF16)


def _ffnconv_call(u, ffn_conv_w, l):
    tm, tn = DEC_SEQ, 256
    nb = D_FF // tn
    w9 = ffn_conv_w.reshape(DEPTH, 9, 2 * D_FF)
    return pl.pallas_call(
        _ffnconv_kernel,
        grid=(N_TOK // tm, nb),
        in_specs=[
            pl.BlockSpec((tm, tn), lambda i, j: (i, j)),
            pl.BlockSpec((tm, tn), lambda i, j: (i, nb + j)),
            pl.BlockSpec((None, 9, tn), lambda i, j: (l, 0, j)),
            pl.BlockSpec((None, 9, tn), lambda i, j: (l, 0, nb + j)),
        ],
        out_specs=pl.BlockSpec((tm, tn), lambda i, j: (i, j)),
        out_shape=jax.ShapeDtypeStruct((N_TOK, D_FF), BF16),
        compiler_params=_params("arbitrary", "arbitrary"),
        name="ffnconv",
    )(u, u, w9, w9)


def _final_norm_kernel(x_ref, w_ref, o_ref):
    x = x_ref[...]
    ms = jnp.mean(x * x, axis=-1, keepdims=True)
    o_ref[...] = x * lax.rsqrt(ms + EPS) * w_ref[...]


def _final_norm_call(x_all, w, row0, n_rows):
    tm = 512
    return pl.pallas_call(
        _final_norm_kernel,
        grid=(n_rows // tm,),
        in_specs=[
            pl.BlockSpec((tm, D_MODEL), lambda i: (row0 // tm + i, 0)),
            pl.BlockSpec((1, D_MODEL), lambda i: (0, 0)),
        ],
        out_specs=pl.BlockSpec((tm, D_MODEL), lambda i: (i, 0)),
        out_shape=jax.ShapeDtypeStruct((n_rows, D_MODEL), F32),
        compiler_params=_params("arbitrary"),
        name="final_norm",
    )(x_all, w.reshape(1, D_MODEL))


def _per_group_lanes(a):
    lead = a.shape[:-2]
    a = a.reshape(*lead, 2, N_GROUPS, HEADS_PER_GROUP)
    a = jnp.moveaxis(a, -3, -2).reshape(*lead, N_GROUPS, 2 * HEADS_PER_GROUP)
    a = jnp.pad(a, [(0, 0)] * (len(lead) + 1) + [(0, LANES - 2 * HEADS_PER_GROUP)])
    return a.reshape(*lead, N_GROUPS * LANES)


def kernel(x_prompt, x_sample, state_ssm, c, c_ctx, w_ada, b_ada, norm_w, w_in, conv_w, conv_b, a_log,
           dt_bias, d_skip, ssm_norm_w, w_out_ssm, sc_conv_w, w_out_sc, w_o, w_up, ffn_conv_w, w_down,
           final_norm_w):
    x_all = jnp.concatenate([x_prompt.reshape(N_CTX_TOK, D_MODEL), x_sample.reshape(N_LAT_TOK, D_MODEL)], axis=0)
    cvec = jnp.zeros((MOD_ROWS, D_MODEL), F32).at[0].set(c_ctx).at[1:1 + DEC_BATCH].set(c)
    mod = _mod_call(cvec, w_ada, b_ada)

    w_dt = _per_group_lanes(w_in[:, :, COL_DT:COL_P2].reshape(DEPTH, D_MODEL, 2, N_HEADS))
    dtb_x = _per_group_lanes(dt_bias).reshape(DEPTH, 1, N_GROUPS * LANES)
    alog_x = _per_group_lanes(a_log).reshape(DEPTH, 1, N_GROUPS * LANES)
    dsk = jnp.repeat(d_skip, HEAD_DIM, axis=-1).reshape(DEPTH, 1, D_INNER)
    conv_b3 = conv_b.reshape(DEPTH, 1, CONV_DIM)
    ssm_nw3 = ssm_norm_w.reshape(DEPTH, 1, D_INNER)

    tm = 1024
    gate_cb = lambda off: (lambda j, i: (i, off + j))
    states = []
    for l in range(DEPTH):
        gmod = lambda which: pl.BlockSpec((None, MOD_ROWS, 1024), lambda j, i: (l, 0, (2 + 3 * which) * 2 + j))
        resid = lambda acc, i, x_ref, g_ref: x_ref[...] + g_ref[pl.ds(_group_of_tile(i, tm), 1), :] * acc

        h = _prenorm_call(x_all, norm_w, mod, l, 0)
        zx = _ws_matmul(h, w_in, l, 0, COL_DT, tm, 1024, F32, "inproj_zx")
        dt_raw = _ws_matmul(h, w_dt, l, 0, N_GROUPS * LANES, tm, 1024, F32, "inproj_dt")
        p2 = _ws_matmul(h, w_in, l, COL_P2, P2_COLS, tm, 1024, F32, "inproj_sc")
        yn, st = _ssd_call(zx, dt_raw, conv_w, conv_b3, dtb_x, alog_x, dsk, ssm_nw3, state_ssm, l)
        states.append(st)
        ms = _ws_matmul(
            yn, w_out_ssm, l, 0, D_MODEL, tm, 512, F32, "out_ssm",
            extra=(p2,), extra_specs=(pl.BlockSpec((tm, 512), gate_cb(3 * D_MODEL // 512)),),
            epilogue=lambda acc, i, g_ref: _sigmoid(g_ref[...]) * acc)
        u_sc = _scgate_call(p2, sc_conv_w, l)
        m = _ws_matmul(
            u_sc, w_out_sc, l, 0, D_MODEL, tm, 512, BF16, "out_sc",
            extra=(ms, p2),
            extra_specs=(pl.BlockSpec((tm, 512), lambda j, i: (i, j)),
                         pl.BlockSpec((tm, 512), gate_cb(4 * D_MODEL // 512))),
            epilogue=lambda acc, i, ms_ref, g_ref: ms_ref[...] + _sigmoid(g_ref[...]) * acc)
        x_all = _ws_matmul(
            m, w_o, l, 0, D_MODEL, tm, 1024, F32, "merge_o",
            extra=(x_all, mod), extra_specs=(pl.BlockSpec((tm, 1024), lambda j, i: (i, j)), gmod(0)),
            epilogue=resid)

        h2 = _prenorm_call(x_all, norm_w, mod, l, 1)
        u = _ws_matmul(h2, w_up, l, 0, 2 * D_FF, tm, 1024, F32, "ffn_up")
        act = _ffnconv_call(u, ffn_conv_w, l)
        tmd = 512
        resid_d = lambda acc, i, x_ref, g_ref: x_ref[...] + g_ref[pl.ds(_group_of_tile(i, tmd), 1), :] * acc
        x_all = _ws_matmul(
            act, w_down, l, 0, D_MODEL, tmd, 512, F32, "ffn_down",
            extra=(x_all, mod),
            extra_specs=(pl.BlockSpec((tmd, 512), lambda j, i: (i, j)),
                         pl.BlockSpec((None, MOD_ROWS, 512), lambda j, i: (l, 0, 5 * (D_MODEL // 512) + j))),
            epilogue=resid_d)

    y_prompt = _final_norm_call(x_all, final_norm_w, 0, N_CTX_TOK).reshape(BATCH, SEQ, D_MODEL)
    y_sample = _final_norm_call(x_all, final_norm_w, N_CTX_TOK, N_LAT_TOK).reshape(DEC_BATCH, DEC_SEQ, D_MODEL)
    new_state = jnp.stack(states, axis=1).reshape(BATCH, DEPTH, 2, N_HEADS, HEAD_DIM, D_STATE)
    return (y_prompt, y_sample, new_state)
```

```python
import functools

import jax
import jax.numpy as jnp
from jax import lax
from jax.experimental import pallas as pl
from jax.experimental.pallas import tpu as pltpu

F32 = jnp.float32
BF16 = jnp.bfloat16

D_MODEL = 2048
BATCH = 32
SEQ = 256
DEPTH = 2
DEC_BATCH = 2
DEC_SEQ = 2048
GRID_W = 64
D_INNER = 4096
HEAD_DIM = 64
N_HEADS = 64
N_GROUPS = 8
HEADS_PER_GROUP = N_HEADS // N_GROUPS
GROUP_DIM = D_INNER // N_GROUPS
D_STATE = 128
CHUNK = 128
CONV_DIM = D_INNER + 2 * N_GROUPS * D_STATE
D_FF = 5632
EPS = 1e-6
N_CTX_TOK = BATCH * SEQ
N_LAT_TOK = DEC_BATCH * DEC_SEQ
N_TOK = N_CTX_TOK + N_LAT_TOK
COL_DT = D_INNER + CONV_DIM
COL_SC = COL_DT + 2 * N_HEADS
COL_GATES = COL_SC + 3 * D_MODEL
MOD_ROWS = 16
SSD_BLOCK = 2048
LANES = 128
VMEM_LIMIT = 60 * 1024 * 1024


def _sigmoid(v):
    return 1.0 / (1.0 + jnp.exp(-v))


def _silu(v):
    return v * _sigmoid(v)


def _softplus(v):
    return jnp.maximum(v, 0.0) + jnp.log1p(jnp.exp(-jnp.abs(v)))


def _group_of_tile(i, tm):
    n_ctx = N_CTX_TOK // tm
    per_lat = DEC_SEQ // tm
    return jnp.where(i < n_ctx, 0, 1 + lax.div(jnp.maximum(i - n_ctx, 0), per_lat))


def _row_conv3(x, w, last):
    n = x.shape[0]
    pos = lax.broadcasted_iota(jnp.int32, x.shape, 0) & last
    prev = jnp.where(pos == 0, 0.0, pltpu.roll(x, 1, 0))
    nxt = jnp.where(pos == last, 0.0, pltpu.roll(x, n - 1, 0))
    return prev * w[0:1] + x * w[1:2] + nxt * w[2:3]


def _params(*sem):
    return pltpu.CompilerParams(dimension_semantics=sem, vmem_limit_bytes=VMEM_LIMIT)


def _mod_kernel(c_ref, w_ref, b_ref, o_ref):
    a = _silu(c_ref[...]).astype(BF16)
    o_ref[...] = jnp.dot(a, w_ref[...].astype(BF16), preferred_element_type=F32) + b_ref[...]


def _mod_call(cvec, w_ada, b_ada):
    tn = 1024
    n = 6 * D_MODEL
    return pl.pallas_call(
        _mod_kernel,
        grid=(DEPTH, n // tn),
        in_specs=[
            pl.BlockSpec((MOD_ROWS, D_MODEL), lambda l, j: (0, 0)),
            pl.BlockSpec((None, D_MODEL, tn), lambda l, j: (l, 0, j)),
            pl.BlockSpec((None, 1, tn), lambda l, j: (l, 0, j)),
        ],
        out_specs=pl.BlockSpec((None, MOD_ROWS, tn), lambda l, j: (l, 0, j)),
        out_shape=jax.ShapeDtypeStruct((DEPTH, MOD_ROWS, n), F32),
        compiler_params=_params("arbitrary", "arbitrary"),
        name="adaln_mod",
    )(cvec, w_ada, b_ada.reshape(DEPTH, 1, n))


def _prenorm_kernel(x_ref, nw_ref, sh_ref, sc_ref, o_ref, *, tm):
    g = _group_of_tile(pl.program_id(0), tm)
    x = x_ref[...]
    ms = jnp.mean(x * x, axis=-1, keepdims=True)
    y = x * lax.rsqrt(ms + EPS) * nw_ref[...]
    o_ref[...] = (y * (1.0 + sc_ref[pl.ds(g, 1), :]) + sh_ref[pl.ds(g, 1), :]).astype(BF16)


def _prenorm_call(x_all, norm_w, mod, l, which):
    tm = 512
    return pl.pallas_call(
        functools.partial(_prenorm_kernel, tm=tm),
        grid=(N_TOK // tm,),
        in_specs=[
            pl.BlockSpec((tm, D_MODEL), lambda i: (i, 0)),
            pl.BlockSpec((None, 1, D_MODEL), lambda i: (2 * l + which, 0, 0)),
            pl.BlockSpec((None, MOD_ROWS, D_MODEL), lambda i: (l, 0, 3 * which)),
            pl.BlockSpec((None, MOD_ROWS, D_MODEL), lambda i: (l, 0, 3 * which + 1)),
        ],
        out_specs=pl.BlockSpec((tm, D_MODEL), lambda i: (i, 0)),
        out_shape=jax.ShapeDtypeStruct((N_TOK, D_MODEL), BF16),
        compiler_params=_params("arbitrary"),
        name=f"prenorm{which}",
    )(x_all, norm_w.reshape(DEPTH * 2, 1, D_MODEL), mod, mod)


def _w_spec(k, tn, l, col_off):
    if col_off % tn == 0:
        return pl.BlockSpec((None, k, tn), lambda j, i: (l, 0, col_off // tn + j))
    return pl.BlockSpec((pl.Element(1), pl.Element(k), pl.Element(tn)),
                        lambda j, i: (l, 0, pl.multiple_of(col_off + j * tn, LANES)))


def _cast_weight(w_ref, wb_ref):
    w = w_ref[0] if len(w_ref.shape) == 3 else w_ref[...]
    wb_ref[...] = w.astype(BF16)


def _mm_kernel(*refs, n_extra, epilogue):
    h_ref, w_ref = refs[0], refs[1]
    extra = refs[2:2 + n_extra]
    o_ref, wb_ref = refs[2 + n_extra], refs[3 + n_extra]
    j, i = pl.program_id(0), pl.program_id(1)

    @pl.when(i == 0)
    def _():
        _cast_weight(w_ref, wb_ref)

    acc = jnp.dot(h_ref[...], wb_ref[...], preferred_element_type=F32)
    epilogue(acc, o_ref, j, i, *extra)


def _store(acc, o_ref, j, i):
    o_ref[...] = acc.astype(o_ref.dtype)


def _ws_matmul(h, w, l, col_off, n_cols, tm, tn, out_dtype, name, extra=(), extra_specs=(), epilogue=_store):
    m, k = h.shape
    return pl.pallas_call(
        functools.partial(_mm_kernel, n_extra=len(extra), epilogue=epilogue),
        grid=(n_cols // tn, m // tm),
        in_specs=[pl.BlockSpec((tm, k), lambda j, i: (i, 0)), _w_spec(k, tn, l, col_off), *extra_specs],
        out_specs=pl.BlockSpec((tm, tn), lambda j, i: (i, j)),
        out_shape=jax.ShapeDtypeStruct((m, n_cols), out_dtype),
        scratch_shapes=[pltpu.VMEM((k, tn), BF16)],
        compiler_params=_params("arbitrary", "arbitrary"),
        name=name,
    )(h, w, *extra)


def _silu_store(acc, o_ref, j, i):
    o_ref[...] = _silu(acc)


def _sigmoid_store(acc, o_ref, j, i):
    o_ref[...] = _sigmoid(acc)


def _conv_silu_store(acc, o_ref, j, i, cw_ref, cb_ref, *, tm):
    last = jnp.where(i < N_CTX_TOK // tm, SEQ - 1, GRID_W - 1)
    o_ref[...] = _silu(_row_conv3(acc, cw_ref[...], last) + cb_ref[...])


def _scu_kernel(h_ref, wb_ref, wc_ref, wh_ref, cw_ref, o_ref, wb_s, wc_s, wh_s, *, tm):
    i = pl.program_id(1)

    @pl.when(i == 0)
    def _():
        _cast_weight(wb_ref, wb_s)
        _cast_weight(wc_ref, wc_s)
        _cast_weight(wh_ref, wh_s)

    h = h_ref[...]
    v = jnp.dot(h, wc_s[...], preferred_element_type=F32) * jnp.dot(h, wh_s[...], preferred_element_type=F32)
    last = jnp.where(i < N_CTX_TOK // tm, SEQ - 1, GRID_W - 1)
    o_ref[...] = (jnp.dot(h, wb_s[...], preferred_element_type=F32) * _row_conv3(v, cw_ref[...], last)).astype(BF16)


def _scu_call(h, w_in, sc_conv_w, l):
    tm, tn = 1024, 512
    return pl.pallas_call(
        functools.partial(_scu_kernel, tm=tm),
        grid=(D_MODEL // tn, N_TOK // tm),
        in_specs=[
            pl.BlockSpec((tm, D_MODEL), lambda j, i: (i, 0)),
            _w_spec(D_MODEL, tn, l, COL_SC),
            _w_spec(D_MODEL, tn, l, COL_SC + D_MODEL),
            _w_spec(D_MODEL, tn, l, COL_SC + 2 * D_MODEL),
            pl.BlockSpec((None, 3, tn), lambda j, i: (l, 0, j)),
        ],
        out_specs=pl.BlockSpec((tm, tn), lambda j, i: (i, j)),
        out_shape=jax.ShapeDtypeStruct((N_TOK, D_MODEL), BF16),
        scratch_shapes=[pltpu.VMEM((D_MODEL, tn), BF16)] * 3,
        compiler_params=_params("arbitrary", "arbitrary"),
        name="inproj_scu",
    )(h, w_in, w_in, w_in, sc_conv_w)


def _grid_conv(x, w, row_len, vertical):
    n = x.shape[0]
    t = lax.broadcasted_iota(jnp.int32, x.shape, 0)
    pos = t & (row_len - 1)
    xl = jnp.where(pos == 0, 0.0, pltpu.roll(x, 1, 0))
    xr = jnp.where(pos == row_len - 1, 0.0, pltpu.roll(x, n - 1, 0))
    row = lambda a: xl * w[3 * a:3 * a + 1] + x * w[3 * a + 1:3 * a + 2] + xr * w[3 * a + 2:3 * a + 3]
    y = row(1)
    if vertical:
        y = y + jnp.where(t >= row_len, pltpu.roll(row(0), row_len, 0), 0.0)
        y = y + jnp.where(t < n - row_len, pltpu.roll(row(2), n - row_len, 0), 0.0)
    return y


def _ffn_up_kernel(*refs, row_len, vertical, aliased):
    h_ref, wg_ref, wv_ref, cg_ref, cv_ref = refs[:5]
    o_ref, wg_s, wv_s = refs[5 + int(aliased):]

    @pl.when(pl.program_id(1) == 0)
    def _():
        _cast_weight(wg_ref, wg_s)
        _cast_weight(wv_ref, wv_s)

    h = h_ref[...]
    g = jnp.dot(h, wg_s[...], preferred_element_type=F32)
    v = jnp.dot(h, wv_s[...], preferred_element_type=F32)
    o_ref[...] = (_silu(_grid_conv(g, cg_ref[...], row_len, vertical))
                  * _grid_conv(v, cv_ref[...], row_len, vertical)).astype(BF16)


def _ffn_up_call(h, w_up, ffn_conv_w, l, prev_act):
    tm, tn = DEC_SEQ, 512
    nb = D_FF // tn
    w9 = ffn_conv_w.reshape(DEPTH, 9, 2 * D_FF)
    latent = prev_act is not None
    tile0, n_tiles = (N_CTX_TOK // tm, N_LAT_TOK // tm) if latent else (0, N_CTX_TOK // tm)
    inputs = [h, w_up, w_up, w9, w9]
    in_specs = [
        pl.BlockSpec((tm, D_MODEL), lambda j, i: (tile0 + i, 0)),
        _w_spec(D_MODEL, tn, l, 0),
        _w_spec(D_MODEL, tn, l, D_FF),
        pl.BlockSpec((None, 9, tn), lambda j, i: (l, 0, j)),
        pl.BlockSpec((None, 9, tn), lambda j, i: (l, 0, nb + j)),
    ]
    if latent:
        inputs.append(prev_act)
        in_specs.append(pl.BlockSpec(memory_space=pl.ANY))
    return pl.pallas_call(
        functools.partial(_ffn_up_kernel, row_len=GRID_W if latent else SEQ, vertical=latent, aliased=latent),
        grid=(nb, n_tiles),
        in_specs=in_specs,
        out_specs=pl.BlockSpec((tm, tn), lambda j, i: (tile0 + i, j)),
        out_shape=jax.ShapeDtypeStruct((N_TOK, D_FF), BF16),
        scratch_shapes=[pltpu.VMEM((D_MODEL, tn), BF16)] * 2,
        input_output_aliases={len(inputs) - 1: 0} if latent else {},
        compiler_params=_params("arbitrary", "arbitrary"),
        name="ffn_up_lat" if latent else "ffn_up_ctx",
    )(*inputs)


def _cumsum_rows(a, qi, reverse):
    s = 1
    while s < CHUNK:
        if reverse:
            a = a + jnp.where(qi < CHUNK - s, pltpu.roll(a, CHUNK - s, 0), 0.0)
        else:
            a = a + jnp.where(qi >= s, pltpu.roll(a, s, 0), 0.0)
        s *= 2
    return a


def _lane_bcast(a, j):
    return jnp.broadcast_to(a[:, j:j + 1], (a.shape[0], LANES))


def _ssd_kernel(*refs, aliased):
    (zs_ref, xs_ref, b_ref, c_ref, dt_ref, dtb_ref, al_ref, dsk_ref, nw_ref, h0_ref) = refs[:10]
    y_ref, st_ref, bt_s, cm_s, dtv_s, yacc_s, st_s = refs[10 + int(aliased):]
    sb = pl.program_id(1)
    a_row = -jnp.exp(al_ref[...])
    qi = lax.broadcasted_iota(jnp.int32, (CHUNK, CHUNK), 0)
    ki = lax.broadcasted_iota(jnp.int32, (CHUNK, CHUNK), 1)
    lo = ki < HEAD_DIM
    lo_row = lo[0:1]
    n_lanes_used = 2 * HEADS_PER_GROUP

    def prep(p, carry):
        rows = pl.ds(pl.multiple_of(p * CHUNK, CHUNK), CHUNK)
        bt_s[p] = b_ref[rows, :].T
        cm_s[rows, :] = c_ref[rows, :].astype(BF16)
        dtv_s[rows, :] = _softplus(dt_ref[rows, :] + dtb_ref[...])
        return carry
    lax.fori_loop(0, SSD_BLOCK // CHUNK, prep, 0)

    def chunk(c, d):
        rows = pl.ds(pl.multiple_of(c * CHUNK, CHUNK), CHUNK)
        dtv = dtv_s[rows, :]
        cum = _cumsum_rows(dtv * a_row, qi, reverse=(d == 1))
        last = cum[CHUNK - 1:CHUNK, :] if d == 0 else cum[0:1, :]
        cum_t = cum.T[0:n_lanes_used]
        dt_t = dtv.T[0:n_lanes_used]
        last_t = cum_t[:, CHUNK - 1:CHUNK] if d == 0 else cum_t[:, 0:1]
        w_t = jnp.exp(last_t - cum_t) * dt_t
        cc_ = cm_s[rows, :]
        bt = bt_s[c]
        cb = jnp.dot(cc_, bt.astype(BF16), preferred_element_type=F32)
        s_in = st_s[...]
        y_off = jnp.dot(cc_, s_in.astype(BF16), preferred_element_type=F32)
        xc = xs_ref[rows, :]
        tri = (qi >= ki) if d == 0 else (qi <= ki)
        y_tiles = []
        for t in range(GROUP_DIM // LANES):
            tile = slice(t * LANES, (t + 1) * LANES)
            xt = xc[:, tile]
            ha, hb = HEADS_PER_GROUP * d + 2 * t, HEADS_PER_GROUP * d + 2 * t + 1
            cols, tops, bots = [], [], []
            for hl in (ha, hb):
                col = _lane_bcast(cum, hl)
                decay = jnp.exp(jnp.where(tri, col - cum_t[hl:hl + 1, :], -jnp.inf))
                cols.append(col)
                tops.append((cb * decay * dt_t[hl:hl + 1, :]).astype(BF16))
                bots.append((bt * w_t[hl:hl + 1, :]).astype(BF16))
            lhs = jnp.concatenate([jnp.concatenate(tops, axis=1), jnp.concatenate(bots, axis=1)], axis=0)
            rhs = jnp.concatenate([jnp.where(lo, xt, 0.0), jnp.where(lo, 0.0, xt)], axis=0).astype(BF16)
            out = jnp.dot(lhs, rhs, preferred_element_type=F32)
            e_in = jnp.exp(jnp.where(lo, cols[0], cols[1]))
            y_tiles.append(out[0:CHUNK] + y_off[:, tile] * e_in)
            dec = jnp.exp(jnp.where(lo_row, _lane_bcast(last, ha), _lane_bcast(last, hb)))
            st_s[:, tile] = s_in[:, tile] * dec + out[CHUNK:2 * CHUNK]
        if d == 0:
            yacc_s[rows, :] = jnp.concatenate(y_tiles, axis=1)
        else:
            y = yacc_s[rows, :] + jnp.concatenate(y_tiles, axis=1) + dsk_ref[...] * xc
            yz = y * zs_ref[rows, :]
            msq = jnp.mean(yz * yz, axis=-1, keepdims=True)
            y_ref[rows, :] = (yz * lax.rsqrt(msq + EPS) * nw_ref[...]).astype(BF16)

    @pl.when(sb < N_CTX_TOK // SSD_BLOCK)
    def _context():
        nc = SEQ // CHUNK

        def seq_body(s, carry):
            st_s[...] = jnp.zeros_like(st_s)
            for k in range(nc):
                chunk(s * nc + k, 0)
            st_ref[s, 0] = st_s[...].T
            st_s[...] = jnp.zeros_like(st_s)
            for k in range(nc - 1, -1, -1):
                chunk(s * nc + k, 1)
            st_ref[s, 1] = st_s[...].T
            return carry
        lax.fori_loop(0, SSD_BLOCK // SEQ, seq_body, 0)

    @pl.when(sb >= N_CTX_TOK // SSD_BLOCK)
    def _latent():
        nc = DEC_SEQ // CHUNK
        st_s[...] = h0_ref[0].T

        def fwd(c, carry):
            chunk(c, 0)
            return carry
        lax.fori_loop(0, nc, fwd, 0)
        st_s[...] = h0_ref[1].T

        def bwd(k, carry):
            chunk(nc - 1 - k, 1)
            return carry
        lax.fori_loop(0, nc, bwd, 0)


def _ssd_call(zs, xbc, dt_raw, dtb_x, alog_x, dsk, ssm_norm_w, state_ssm, prev_states, l):
    n_ctx_blk = N_CTX_TOK // SSD_BLOCK
    n_blk = N_TOK // SSD_BLOCK
    seq_per_blk = SSD_BLOCK // SEQ
    b_cb = D_INNER // D_STATE
    c_cb = b_cb + N_GROUPS
    rowvec = lambda width: pl.BlockSpec((None, 1, width), lambda g, sb: (l, 0, g))
    h0 = state_ssm.reshape(DEC_BATCH, DEPTH, 2, N_GROUPS, GROUP_DIM, D_STATE)
    aliased = prev_states is not None
    inputs = [zs, xbc, xbc, xbc, dt_raw, dtb_x, alog_x, dsk, ssm_norm_w, h0]
    in_specs = [
        pl.BlockSpec((SSD_BLOCK, GROUP_DIM), lambda g, sb: (sb, g)),
        pl.BlockSpec((SSD_BLOCK, GROUP_DIM), lambda g, sb: (sb, g)),
        pl.BlockSpec((SSD_BLOCK, D_STATE), lambda g, sb: (sb, b_cb + g)),
        pl.BlockSpec((SSD_BLOCK, D_STATE), lambda g, sb: (sb, c_cb + g)),
        pl.BlockSpec((SSD_BLOCK, LANES), lambda g, sb: (sb, g)),
        rowvec(LANES), rowvec(LANES), rowvec(GROUP_DIM), rowvec(GROUP_DIM),
        pl.BlockSpec((None, None, 2, None, GROUP_DIM, D_STATE),
                     lambda g, sb: (jnp.maximum(sb - n_ctx_blk, 0), l, 0, g, 0, 0)),
    ]
    if aliased:
        inputs.append(prev_states)
        in_specs.append(pl.BlockSpec(memory_space=pl.ANY))
    return pl.pallas_call(
        functools.partial(_ssd_kernel, aliased=aliased),
        grid=(N_GROUPS, n_blk),
        in_specs=in_specs,
        out_specs=[
            pl.BlockSpec((SSD_BLOCK, GROUP_DIM), lambda g, sb: (sb, g)),
            pl.BlockSpec((seq_per_blk, None, 2, None, GROUP_DIM, D_STATE),
                         lambda g, sb: (jnp.minimum(sb, n_ctx_blk - 1), l, 0, g, 0, 0)),
        ],
        out_shape=[
            jax.ShapeDtypeStruct((N_TOK, D_INNER), BF16),
            jax.ShapeDtypeStruct((BATCH, DEPTH, 2, N_GROUPS, GROUP_DIM, D_STATE), F32),
        ],
        scratch_shapes=[
            pltpu.VMEM((SSD_BLOCK // CHUNK, D_STATE, CHUNK), F32),
            pltpu.VMEM((SSD_BLOCK, D_STATE), BF16),
            pltpu.VMEM((SSD_BLOCK, LANES), F32),
            pltpu.VMEM((SSD_BLOCK, GROUP_DIM), F32),
            pltpu.VMEM((D_STATE, GROUP_DIM), F32),
        ],
        input_output_aliases={len(inputs) - 1: 1} if aliased else {},
        compiler_params=_params("arbitrary", "arbitrary"),
        name="ssd",
    )(*inputs)


def _final_norm_kernel(x_ref, w_ref, o_ref):
    x = x_ref[...]
    ms = jnp.mean(x * x, axis=-1, keepdims=True)
    o_ref[...] = x * lax.rsqrt(ms + EPS) * w_ref[...]


def _final_norm_call(x_all, w, row0, n_rows):
    tm = 512
    return pl.pallas_call(
        _final_norm_kernel,
        grid=(n_rows // tm,),
        in_specs=[
            pl.BlockSpec((tm, D_MODEL), lambda i: (row0 // tm + i, 0)),
            pl.BlockSpec((1, D_MODEL), lambda i: (0, 0)),
        ],
        out_specs=pl.BlockSpec((tm, D_MODEL), lambda i: (i, 0)),
        out_shape=jax.ShapeDtypeStruct((n_rows, D_MODEL), F32),
        compiler_params=_params("arbitrary"),
        name="final_norm",
    )(x_all, w.reshape(1, D_MODEL))


def _per_group_lanes(a):
    lead = a.shape[:-2]
    a = a.reshape(*lead, 2, N_GROUPS, HEADS_PER_GROUP)
    a = jnp.moveaxis(a, -3, -2).reshape(*lead, N_GROUPS, 2 * HEADS_PER_GROUP)
    a = jnp.pad(a, [(0, 0)] * (len(lead) + 1) + [(0, LANES - 2 * HEADS_PER_GROUP)])
    return a.reshape(*lead, N_GROUPS * LANES)


def kernel(x_prompt, x_sample, state_ssm, c, c_ctx, w_ada, b_ada, norm_w, w_in, conv_w, conv_b, a_log,
           dt_bias, d_skip, ssm_norm_w, w_out_ssm, sc_conv_w, w_out_sc, w_o, w_up, ffn_conv_w, w_down,
           final_norm_w):
    x_all = jnp.concatenate([x_prompt.reshape(N_CTX_TOK, D_MODEL), x_sample.reshape(N_LAT_TOK, D_MODEL)], axis=0)
    cvec = jnp.zeros((MOD_ROWS, D_MODEL), F32).at[0].set(c_ctx).at[1:1 + DEC_BATCH].set(c)
    mod = _mod_call(cvec, w_ada, b_ada)

    w_dt = _per_group_lanes(w_in[:, :, COL_DT:COL_SC].reshape(DEPTH, D_MODEL, 2, N_HEADS))
    dtb_x = _per_group_lanes(dt_bias).reshape(DEPTH, 1, N_GROUPS * LANES)
    alog_x = _per_group_lanes(a_log).reshape(DEPTH, 1, N_GROUPS * LANES)
    dsk = jnp.repeat(d_skip, HEAD_DIM, axis=-1).reshape(DEPTH, 1, D_INNER)
    conv_b3 = conv_b.reshape(DEPTH, 1, CONV_DIM)
    ssm_nw3 = ssm_norm_w.reshape(DEPTH, 1, D_INNER)

    tm = 1024
    states = None
    for l in range(DEPTH):
        def gate_of_mod(which, tn):
            return pl.BlockSpec((None, MOD_ROWS, tn), lambda j, i: (l, 0, (2 + 3 * which) * (D_MODEL // tn) + j))

        def residual(tile_m):
            def epilogue(acc, o_ref, j, i, x_ref, g_ref):
                o_ref[...] = x_ref[...] + g_ref[pl.ds(_group_of_tile(i, tile_m), 1), :] * acc
            return epilogue

        h = _prenorm_call(x_all, norm_w, mod, l, 0)
        zs = _ws_matmul(h, w_in, l, 0, D_INNER, tm, 1024, F32, "inproj_z", epilogue=_silu_store)
        xbc = _ws_matmul(
            h, w_in, l, D_INNER, CONV_DIM, tm, 1024, F32, "inproj_xbc",
            extra=(conv_w, conv_b3),
            extra_specs=(pl.BlockSpec((None, 3, 1024), lambda j, i: (l, 0, j)),
                         pl.BlockSpec((None, 1, 1024), lambda j, i: (l, 0, j))),
            epilogue=functools.partial(_conv_silu_store, tm=tm))
        dt_raw = _ws_matmul(h, w_dt, l, 0, N_GROUPS * LANES, tm, 1024, F32, "inproj_dt")
        u_sc = _scu_call(h, w_in, sc_conv_w, l)
        sg = _ws_matmul(h, w_in, l, COL_GATES, 2 * D_MODEL, tm, 1024, F32, "inproj_gates", epilogue=_sigmoid_store)

        yn, states = _ssd_call(zs, xbc, dt_raw, dtb_x, alog_x, dsk, ssm_nw3, state_ssm, states, l)

        def gate_ssm(acc, o_ref, j, i, g_ref):
            o_ref[...] = g_ref[...] * acc
        ms = _ws_matmul(
            yn, w_out_ssm, l, 0, D_MODEL, tm, 512, F32, "out_ssm",
            extra=(sg,), extra_specs=(pl.BlockSpec((tm, 512), lambda j, i: (i, j)),), epilogue=gate_ssm)

        def gate_sc(acc, o_ref, j, i, ms_ref, g_ref):
            o_ref[...] = (ms_ref[...] + g_ref[...] * acc).astype(BF16)
        m = _ws_matmul(
            u_sc, w_out_sc, l, 0, D_MODEL, tm, 512, BF16, "out_sc",
            extra=(ms, sg),
            extra_specs=(pl.BlockSpec((tm, 512), lambda j, i: (i, j)),
                         pl.BlockSpec((tm, 512), lambda j, i: (i, D_MODEL // 512 + j))),
            epilogue=gate_sc)
        x_all = _ws_matmul(
            m, w_o, l, 0, D_MODEL, tm, 1024, F32, "merge_o",
            extra=(x_all, mod),
            extra_specs=(pl.BlockSpec((tm, 1024), lambda j, i: (i, j)), gate_of_mod(0, 1024)),
            epilogue=residual(tm))

        h2 = _prenorm_call(x_all, norm_w, mod, l, 1)
        act = _ffn_up_call(h2, w_up, ffn_conv_w, l, None)
        act = _ffn_up_call(h2, w_up, ffn_conv_w, l, act)
        tmd = 512
        x_all = _ws_matmul(
            act, w_down, l, 0, D_MODEL, tmd, 512, F32, "ffn_down",
            extra=(x_all, mod),
            extra_specs=(pl.BlockSpec((tmd, 512), lambda j, i: (i, j)), gate_of_mod(1, 512)),
            epilogue=residual(tmd))

    y_prompt = _final_norm_call(x_all, final_norm_w, 0, N_CTX_TOK).reshape(BATCH, SEQ, D_MODEL)
    y_sample = _final_norm_call(x_all, final_norm_w, N_CTX_TOK, N_LAT_TOK).reshape(DEC_BATCH, DEC_SEQ, D_MODEL)
    new_state = states.reshape(BATCH, DEPTH, 2, N_HEADS, HEAD_DIM, D_STATE)
    return (y_prompt, y_sample, new_state)
```

```python
import functools

import jax
import jax.numpy as jnp
from jax import lax
from jax.experimental import pallas as pl
from jax.experimental.pallas import tpu as pltpu

F32 = jnp.float32
BF16 = jnp.bfloat16

D_MODEL = 2048
BATCH = 32
SEQ = 256
DEPTH = 2
DEC_BATCH = 2
DEC_SEQ = 2048
GRID_W = 64
D_INNER = 4096
HEAD_DIM = 64
N_HEADS = 64
N_GROUPS = 8
HEADS_PER_GROUP = N_HEADS // N_GROUPS
GROUP_DIM = D_INNER // N_GROUPS
D_STATE = 128
CHUNK = 128
CONV_DIM = D_INNER + 2 * N_GROUPS * D_STATE
D_FF = 5632
EPS = 1e-6
LOG2E = 1.4426950408889634
N_CTX_TOK = BATCH * SEQ
N_LAT_TOK = DEC_BATCH * DEC_SEQ
N_TOK = N_CTX_TOK + N_LAT_TOK
COL_DT = D_INNER + CONV_DIM
COL_SC = COL_DT + 2 * N_HEADS
COL_GATES = COL_SC + 3 * D_MODEL
MOD_ROWS = 16
SSD_BLOCK = 2048
FFN_ROWS = 512
LANES = 128
VMEM_LIMIT = 60 * 1024 * 1024


def _sigmoid(v):
    return 1.0 / (1.0 + jnp.exp(-v))


def _silu(v):
    return v * _sigmoid(v)


def _softplus(v):
    return jnp.maximum(v, 0.0) + jnp.log1p(jnp.exp(-jnp.abs(v)))


def _group_of_tile(i, tm):
    n_ctx = N_CTX_TOK // tm
    per_lat = DEC_SEQ // tm
    return jnp.where(i < n_ctx, 0, 1 + lax.div(jnp.maximum(i - n_ctx, 0), per_lat))


def _row_conv3(x, w, last):
    n = x.shape[0]
    pos = lax.broadcasted_iota(jnp.int32, x.shape, 0) & last
    prev = jnp.where(pos == 0, 0.0, pltpu.roll(x, 1, 0))
    nxt = jnp.where(pos == last, 0.0, pltpu.roll(x, n - 1, 0))
    return prev * w[0:1] + x * w[1:2] + nxt * w[2:3]


def _params(*sem):
    return pltpu.CompilerParams(dimension_semantics=sem, vmem_limit_bytes=VMEM_LIMIT)


def _mod_kernel(c_ref, w_ref, b_ref, o_ref):
    a = _silu(c_ref[...]).astype(BF16)
    o_ref[...] = jnp.dot(a, w_ref[...].astype(BF16), preferred_element_type=F32) + b_ref[...]


def _mod_call(cvec, w_ada, b_ada):
    tn = 1024
    n = 6 * D_MODEL
    return pl.pallas_call(
        _mod_kernel,
        grid=(DEPTH, n // tn),
        in_specs=[
            pl.BlockSpec((MOD_ROWS, D_MODEL), lambda l, j: (0, 0)),
            pl.BlockSpec((None, D_MODEL, tn), lambda l, j: (l, 0, j)),
            pl.BlockSpec((None, 1, tn), lambda l, j: (l, 0, j)),
        ],
        out_specs=pl.BlockSpec((None, MOD_ROWS, tn), lambda l, j: (l, 0, j)),
        out_shape=jax.ShapeDtypeStruct((DEPTH, MOD_ROWS, n), F32),
        compiler_params=_params("arbitrary", "arbitrary"),
        name="adaln_mod",
    )(cvec, w_ada, b_ada.reshape(DEPTH, 1, n))


def _x_specs(x, tm, tn, index):
    if not isinstance(x, tuple):
        return (x,), (pl.BlockSpec((tm, tn), index),)
    n_ctx = N_CTX_TOK // tm

    def ctx_index(*grid):
        r, c = index(*grid)
        return jnp.minimum(r, n_ctx - 1), c

    def lat_index(*grid):
        r, c = index(*grid)
        return jnp.maximum(r - n_ctx, 0), c
    return x, (pl.BlockSpec((tm, tn), ctx_index), pl.BlockSpec((tm, tn), lat_index))


def _x_tile(x_refs, i, tm):
    if len(x_refs) == 1:
        return x_refs[0][...]
    return jnp.where(i < N_CTX_TOK // tm, x_refs[0][...], x_refs[1][...])


def _prenorm_kernel(*refs, tm, n_x):
    nw_ref, sh_ref, sc_ref, o_ref = refs[n_x:]
    i = pl.program_id(0)
    g = _group_of_tile(i, tm)
    x = _x_tile(refs[:n_x], i, tm)
    ms = jnp.mean(x * x, axis=-1, keepdims=True)
    y = x * lax.rsqrt(ms + EPS) * nw_ref[...]
    o_ref[...] = (y * (1.0 + sc_ref[pl.ds(g, 1), :]) + sh_ref[pl.ds(g, 1), :]).astype(BF16)


def _prenorm_call(x, norm_w, mod, l, which):
    tm = 512
    xs, x_specs = _x_specs(x, tm, D_MODEL, lambda i: (i, 0))
    return pl.pallas_call(
        functools.partial(_prenorm_kernel, tm=tm, n_x=len(xs)),
        grid=(N_TOK // tm,),
        in_specs=[
            *x_specs,
            pl.BlockSpec((None, 1, D_MODEL), lambda i: (2 * l + which, 0, 0)),
            pl.BlockSpec((None, MOD_ROWS, D_MODEL), lambda i: (l, 0, 3 * which)),
            pl.BlockSpec((None, MOD_ROWS, D_MODEL), lambda i: (l, 0, 3 * which + 1)),
        ],
        out_specs=pl.BlockSpec((tm, D_MODEL), lambda i: (i, 0)),
        out_shape=jax.ShapeDtypeStruct((N_TOK, D_MODEL), BF16),
        compiler_params=_params("arbitrary"),
        name=f"prenorm{which}",
    )(*xs, norm_w.reshape(DEPTH * 2, 1, D_MODEL), mod, mod)


def _w_spec(k, tn, l, col_off):
    if col_off % tn == 0:
        return pl.BlockSpec((None, k, tn), lambda j, i: (l, 0, col_off // tn + j))
    return pl.BlockSpec((pl.Element(1), pl.Element(k), pl.Element(tn)),
                        lambda j, i: (l, 0, pl.multiple_of(col_off + j * tn, LANES)))


def _cast_weight(w_ref, wb_ref):
    w = w_ref[0] if len(w_ref.shape) == 3 else w_ref[...]
    wb_ref[...] = w.astype(BF16)


def _mm_kernel(*refs, n_h, n_extra, epilogue, tm):
    w_ref = refs[n_h]
    extra = refs[n_h + 1:n_h + 1 + n_extra]
    o_ref, wb_ref = refs[n_h + 1 + n_extra:]
    j, i = pl.program_id(0), pl.program_id(1)

    @pl.when(i == 0)
    def _():
        _cast_weight(w_ref, wb_ref)

    acc = jnp.dot(_x_tile(refs[:n_h], i, tm), wb_ref[...], preferred_element_type=F32)
    epilogue(acc, o_ref, j, i, *extra)


def _store(acc, o_ref, j, i):
    o_ref[...] = acc.astype(o_ref.dtype)


def _ws_matmul(h, w, l, col_off, n_cols, tm, tn, out_dtype, name, extra=(), extra_specs=(), epilogue=_store):
    k = h[0].shape[1] if isinstance(h, tuple) else h.shape[1]
    hs, h_specs = _x_specs(h, tm, k, lambda j, i: (i, 0))
    return pl.pallas_call(
        functools.partial(_mm_kernel, n_h=len(hs), n_extra=len(extra), epilogue=epilogue, tm=tm),
        grid=(n_cols // tn, N_TOK // tm),
        in_specs=[*h_specs, _w_spec(k, tn, l, col_off), *extra_specs],
        out_specs=pl.BlockSpec((tm, tn), lambda j, i: (i, j)),
        out_shape=jax.ShapeDtypeStruct((N_TOK, n_cols), out_dtype),
        scratch_shapes=[pltpu.VMEM((k, tn), BF16)],
        compiler_params=_params("arbitrary", "arbitrary"),
        name=name,
    )(*hs, w, *extra)


def _silu_store(acc, o_ref, j, i):
    o_ref[...] = _silu(acc)


def _sigmoid_store(acc, o_ref, j, i):
    o_ref[...] = _sigmoid(acc)


def _conv_silu_store(acc, o_ref, j, i, cw_ref, cb_ref, *, tm):
    last = jnp.where(i < N_CTX_TOK // tm, SEQ - 1, GRID_W - 1)
    o_ref[...] = _silu(_row_conv3(acc, cw_ref[...], last) + cb_ref[...])


def _scu_kernel(h_ref, wb_ref, wc_ref, wh_ref, cw_ref, o_ref, wb_s, wc_s, wh_s, *, tm):
    i = pl.program_id(1)

    @pl.when(i == 0)
    def _():
        _cast_weight(wb_ref, wb_s)
        _cast_weight(wc_ref, wc_s)
        _cast_weight(wh_ref, wh_s)

    h = h_ref[...]
    v = jnp.dot(h, wc_s[...], preferred_element_type=F32) * jnp.dot(h, wh_s[...], preferred_element_type=F32)
    last = jnp.where(i < N_CTX_TOK // tm, SEQ - 1, GRID_W - 1)
    o_ref[...] = (jnp.dot(h, wb_s[...], preferred_element_type=F32) * _row_conv3(v, cw_ref[...], last)).astype(BF16)


def _scu_call(h, w_in, sc_conv_w, l):
    tm, tn = 1024, 512
    return pl.pallas_call(
        functools.partial(_scu_kernel, tm=tm),
        grid=(D_MODEL // tn, N_TOK // tm),
        in_specs=[
            pl.BlockSpec((tm, D_MODEL), lambda j, i: (i, 0)),
            _w_spec(D_MODEL, tn, l, COL_SC),
            _w_spec(D_MODEL, tn, l, COL_SC + D_MODEL),
            _w_spec(D_MODEL, tn, l, COL_SC + 2 * D_MODEL),
            pl.BlockSpec((None, 3, tn), lambda j, i: (l, 0, j)),
        ],
        out_specs=pl.BlockSpec((tm, tn), lambda j, i: (i, j)),
        out_shape=jax.ShapeDtypeStruct((N_TOK, D_MODEL), BF16),
        scratch_shapes=[pltpu.VMEM((D_MODEL, tn), BF16)] * 3,
        compiler_params=_params("arbitrary", "arbitrary"),
        name="inproj_scu",
    )(h, w_in, w_in, w_in, sc_conv_w)


def _branch_mix_kernel(yn_ref, u_ref, wa_ref, wb_ref, ga_ref, gb_ref, o_ref, wa_s, wb_s):
    @pl.when(pl.program_id(1) == 0)
    def _():
        _cast_weight(wa_ref, wa_s)
        _cast_weight(wb_ref, wb_s)

    y_ssm = jnp.dot(yn_ref[...], wa_s[...], preferred_element_type=F32)
    y_sc = jnp.dot(u_ref[...], wb_s[...], preferred_element_type=F32)
    o_ref[...] = (ga_ref[...] * y_ssm + gb_ref[...] * y_sc).astype(BF16)


def _branch_mix_call(yn, u_sc, w_out_ssm, w_out_sc, sg, l):
    tm, tn = 512, 512
    nb = D_MODEL // tn
    return pl.pallas_call(
        _branch_mix_kernel,
        grid=(nb, N_TOK // tm),
        in_specs=[
            pl.BlockSpec((tm, D_INNER), lambda j, i: (i, 0)),
            pl.BlockSpec((tm, D_MODEL), lambda j, i: (i, 0)),
            _w_spec(D_INNER, tn, l, 0),
            _w_spec(D_MODEL, tn, l, 0),
            pl.BlockSpec((tm, tn), lambda j, i: (i, j)),
            pl.BlockSpec((tm, tn), lambda j, i: (i, nb + j)),
        ],
        out_specs=pl.BlockSpec((tm, tn), lambda j, i: (i, j)),
        out_shape=jax.ShapeDtypeStruct((N_TOK, D_MODEL), BF16),
        scratch_shapes=[pltpu.VMEM((D_INNER, tn), BF16), pltpu.VMEM((D_MODEL, tn), BF16)],
        compiler_params=_params("arbitrary", "arbitrary"),
        name="branch_mix",
    )(yn, u_sc, w_out_ssm, w_out_sc, sg, sg)


def _ffn_up_kernel(h_ref, wg_ref, wv_ref, cg_ref, cv_ref, o_ref, wg_s, wv_s, g_s, v_s, *, row_len, vertical):
    tm = h_ref.shape[0]
    halo = row_len if vertical else 0

    @pl.when(pl.program_id(1) == 0)
    def _():
        _cast_weight(wg_ref, wg_s)
        _cast_weight(wv_ref, wv_s)
        if halo:
            for s in (g_s, v_s):
                s[0:halo, :] = jnp.zeros((halo, s.shape[1]), F32)
                s[halo + tm:, :] = jnp.zeros((halo, s.shape[1]), F32)

    def conv(s_ref, w, r0):
        ext = s_ref[r0:r0 + FFN_ROWS + 2 * halo, :]
        n = ext.shape[0]
        pos = lax.broadcasted_iota(jnp.int32, ext.shape, 0) & (row_len - 1)
        xl = jnp.where(pos == 0, 0.0, pltpu.roll(ext, 1, 0))
        xr = jnp.where(pos == row_len - 1, 0.0, pltpu.roll(ext, n - 1, 0))
        y = None
        for a in ((0, 1, 2) if vertical else (1,)):
            sl = slice(halo + (a - 1) * row_len, halo + (a - 1) * row_len + FFN_ROWS) if vertical else slice(0, n)
            term = xl[sl] * w[3 * a:3 * a + 1] + ext[sl] * w[3 * a + 1:3 * a + 2] + xr[sl] * w[3 * a + 2:3 * a + 3]
            y = term if y is None else y + term
        return y

    cg, cv = cg_ref[...], cv_ref[...]
    n_blocks = tm // FFN_ROWS
    for b in range(n_blocks + 1):
        if b < n_blocks:
            hb = h_ref[b * FFN_ROWS:(b + 1) * FFN_ROWS, :]
            rows = slice(halo + b * FFN_ROWS, halo + (b + 1) * FFN_ROWS)
            g_s[rows, :] = jnp.dot(hb, wg_s[...], preferred_element_type=F32)
            v_s[rows, :] = jnp.dot(hb, wv_s[...], preferred_element_type=F32)
        if b >= 1:
            r0 = (b - 1) * FFN_ROWS
            o_ref[r0:r0 + FFN_ROWS, :] = (_silu(conv(g_s, cg, r0)) * conv(v_s, cv, r0)).astype(BF16)


def _ffn_up_call(h, w_up, ffn_conv_w, l, latent):
    tm, tn = DEC_SEQ, 512
    nb = D_FF // tn
    w9 = ffn_conv_w.reshape(DEPTH, 9, 2 * D_FF)
    tile0, n_rows = (N_CTX_TOK // tm, N_LAT_TOK) if latent else (0, N_CTX_TOK)
    return pl.pallas_call(
        functools.partial(_ffn_up_kernel, row_len=GRID_W if latent else SEQ, vertical=latent),
        grid=(nb, n_rows // tm),
        in_specs=[
            pl.BlockSpec((tm, D_MODEL), lambda j, i: (tile0 + i, 0)),
            _w_spec(D_MODEL, tn, l, 0),
            _w_spec(D_MODEL, tn, l, D_FF),
            pl.BlockSpec((None, 9, tn), lambda j, i: (l, 0, j)),
            pl.BlockSpec((None, 9, tn), lambda j, i: (l, 0, nb + j)),
        ],
        out_specs=pl.BlockSpec((tm, tn), lambda j, i: (i, j)),
        out_shape=jax.ShapeDtypeStruct((n_rows, D_FF), BF16),
        scratch_shapes=[pltpu.VMEM((D_MODEL, tn), BF16)] * 2
        + [pltpu.VMEM((tm + (2 * GRID_W if latent else 0), tn), F32)] * 2,
        compiler_params=_params("arbitrary", "arbitrary"),
        name="ffn_up_lat" if latent else "ffn_up_ctx",
    )(h, w_up, w_up, w9, w9)


def _prefix_sum_rows(a, qi):
    s = 1
    while s < CHUNK:
        a = a + jnp.where(qi >= s, pltpu.roll(a, s, 0), 0.0)
        s *= 2
    return a


def _lane_bcast(a, j):
    return jnp.broadcast_to(a[:, j:j + 1], (a.shape[0], LANES))


def _ssd_kernel(*refs, aliased, layer):
    (zs_ref, xs_ref, b_ref, c_ref, dt_ref, dtb_ref, al_ref, dsk_ref, nw_ref, h0_ref) = refs[:10]
    (y_ref, st_ref, cum_s, rt_s, wt_s, tot_s, bt_s, cm_s, cb_s, rhs_s,
     yf_s, yb_s, stf_s, stb_s) = refs[10 + int(aliased):]
    sb = pl.program_id(1)
    a2_row = -jnp.exp(al_ref[...]) * LOG2E
    qi = lax.broadcasted_iota(jnp.int32, (CHUNK, CHUNK), 0)
    ki = lax.broadcasted_iota(jnp.int32, (CHUNK, CHUNK), 1)
    lo = ki < HEAD_DIM
    lo_row = lo[0:1]
    n_lanes_used = 2 * HEADS_PER_GROUP
    fwd_lane = ki < HEADS_PER_GROUP
    fwd_row = lax.broadcasted_iota(jnp.int32, (n_lanes_used, 1), 0) < HEADS_PER_GROUP

    def prep(p, carry):
        rows = pl.ds(pl.multiple_of(p * CHUNK, CHUNK), CHUNK)
        dtv = _softplus(dt_ref[rows, :] + dtb_ref[...])
        a = dtv * a2_row
        pre = _prefix_sum_rows(a, qi)
        tot = pre[CHUNK - 1:CHUNK, :]
        cum = jnp.where(fwd_lane, pre, tot - pre + a)
        cum_t = cum.T[0:n_lanes_used]
        dt_t = dtv.T[0:n_lanes_used]
        tot_t = jnp.where(fwd_row, cum_t[:, CHUNK - 1:CHUNK], cum_t[:, 0:1])
        cum_s[p] = cum
        tot_s[p] = jnp.broadcast_to(tot, tot_s.shape[1:])
        rt_s[p] = cum_t - jnp.log2(dt_t)
        wt_s[p] = jnp.exp2(tot_t - cum_t) * dt_t
        bt = b_ref[rows, :].T
        cm = c_ref[rows, :].astype(BF16)
        bt_s[p] = bt
        cm_s[rows, :] = cm
        cb_s[p] = jnp.dot(cm, bt.astype(BF16), preferred_element_type=F32)
        for t in range(GROUP_DIM // LANES):
            xt = xs_ref[rows, t * LANES:(t + 1) * LANES]
            rhs_s[p, t] = jnp.concatenate([jnp.where(lo, xt, 0.0), jnp.where(lo, 0.0, xt)], axis=0).astype(BF16)
        return carry
    lax.fori_loop(0, SSD_BLOCK // CHUNK, prep, 0, unroll=4)

    def chunk_head(c, d, st_s):
        rows = pl.ds(pl.multiple_of(c * CHUNK, CHUNK), CHUNK)
        s_in = st_s[...]
        y_off = jnp.dot(cm_s[rows, :], s_in.astype(BF16), preferred_element_type=F32)
        return dict(c=c, rows=rows, cum=cum_s[c], tot=tot_s[c, 0:1], rt=rt_s[c], wt=wt_s[c], bt=bt_s[c],
                    cb=cb_s[c], s_in=s_in, y_off=y_off, tri=(qi >= ki) if d == 0 else (qi <= ki))

    def head_pair(t, d, p, st_s, y_s):
        tile = slice(t * LANES, (t + 1) * LANES)
        ha, hb = HEADS_PER_GROUP * d + 2 * t, HEADS_PER_GROUP * d + 2 * t + 1
        cols, tops, bots = [], [], []
        for hl in (ha, hb):
            col = _lane_bcast(p["cum"], hl)
            decay_dt = jnp.exp2(jnp.where(p["tri"], col - p["rt"][hl:hl + 1, :], -jnp.inf))
            cols.append(col)
            tops.append((p["cb"] * decay_dt).astype(BF16))
            bots.append((p["bt"] * p["wt"][hl:hl + 1, :]).astype(BF16))
        lhs = jnp.concatenate([jnp.concatenate(tops, axis=1), jnp.concatenate(bots, axis=1)], axis=0)
        out = jnp.dot(lhs, rhs_s[p["c"], t], preferred_element_type=F32)
        e_in = jnp.exp2(jnp.where(lo, cols[0], cols[1]))
        y_s[p["rows"], tile] = out[0:CHUNK] + p["y_off"][:, tile] * e_in
        dec = jnp.exp2(jnp.where(lo_row, _lane_bcast(p["tot"], ha), _lane_bcast(p["tot"], hb)))
        st_s[:, tile] = p["s_in"][:, tile] * dec + out[CHUNK:2 * CHUNK]

    def chunk_both(c_fwd, c_bwd):
        pf = chunk_head(c_fwd, 0, stf_s)
        pb = chunk_head(c_bwd, 1, stb_s)
        for t in range(GROUP_DIM // LANES):
            head_pair(t, 0, pf, stf_s, yf_s)
            head_pair(t, 1, pb, stb_s, yb_s)

    @pl.when(sb < N_CTX_TOK // SSD_BLOCK)
    def _context():
        nc = SEQ // CHUNK

        def seq_body(s, carry):
            stf_s[...] = jnp.zeros_like(stf_s)
            stb_s[...] = jnp.zeros_like(stb_s)
            for k in range(nc):
                chunk_both(s * nc + k, s * nc + nc - 1 - k)
            for d, st_s in enumerate((stf_s, stb_s)):
                if aliased:
                    st_ref[s, d] = st_s[...].T
                else:
                    for other in range(DEPTH):
                        st_ref[s, other, d] = st_s[...].T if other == layer else jnp.zeros(st_ref.shape[3:], F32)
            return carry
        lax.fori_loop(0, SSD_BLOCK // SEQ, seq_body, 0)

    @pl.when(sb >= N_CTX_TOK // SSD_BLOCK)
    def _latent():
        nc = DEC_SEQ // CHUNK
        stf_s[...] = h0_ref[0].T
        stb_s[...] = h0_ref[1].T

        def both(k, carry):
            chunk_both(k, nc - 1 - k)
            return carry
        lax.fori_loop(0, nc, both, 0)

    def combine(c, carry):
        rows = pl.ds(pl.multiple_of(c * CHUNK, CHUNK), CHUNK)
        y = yf_s[rows, :] + yb_s[rows, :] + dsk_ref[...] * xs_ref[rows, :]
        yz = y * zs_ref[rows, :]
        msq = jnp.mean(yz * yz, axis=-1, keepdims=True)
        y_ref[rows, :] = (yz * lax.rsqrt(msq + EPS) * nw_ref[...]).astype(BF16)
        return carry
    lax.fori_loop(0, SSD_BLOCK // CHUNK, combine, 0, unroll=2)


def _ssd_call(zs, xbc, dt_raw, dtb_x, alog_x, dsk, ssm_norm_w, state_ssm, prev_states, l):
    n_ctx_blk = N_CTX_TOK // SSD_BLOCK
    n_blk = N_TOK // SSD_BLOCK
    seq_per_blk = SSD_BLOCK // SEQ
    n_chunks = SSD_BLOCK // CHUNK
    b_cb = D_INNER // D_STATE
    c_cb = b_cb + N_GROUPS
    rowvec = lambda width: pl.BlockSpec((None, 1, width), lambda g, sb: (l, 0, g))
    h0 = state_ssm.reshape(DEC_BATCH, DEPTH, 2, N_GROUPS, GROUP_DIM, D_STATE)
    aliased = prev_states is not None
    inputs = [zs, xbc, xbc, xbc, dt_raw, dtb_x, alog_x, dsk, ssm_norm_w, h0]
    in_specs = [
        pl.BlockSpec((SSD_BLOCK, GROUP_DIM), lambda g, sb: (sb, g), pipeline_mode=pl.Buffered(1)),
        pl.BlockSpec((SSD_BLOCK, GROUP_DIM), lambda g, sb: (sb, g)),
        pl.BlockSpec((SSD_BLOCK, D_STATE), lambda g, sb: (sb, b_cb + g)),
        pl.BlockSpec((SSD_BLOCK, D_STATE), lambda g, sb: (sb, c_cb + g)),
        pl.BlockSpec((SSD_BLOCK, LANES), lambda g, sb: (sb, g)),
        rowvec(LANES), rowvec(LANES), rowvec(GROUP_DIM), rowvec(GROUP_DIM),
        pl.BlockSpec((None, None, 2, None, GROUP_DIM, D_STATE),
                     lambda g, sb: (jnp.maximum(sb - n_ctx_blk, 0), l, 0, g, 0, 0)),
    ]
    if aliased:
        inputs.append(prev_states)
        in_specs.append(pl.BlockSpec(memory_space=pl.ANY))
    return pl.pallas_call(
        functools.partial(_ssd_kernel, aliased=aliased, layer=l),
        grid=(N_GROUPS, n_blk),
        in_specs=in_specs,
        out_specs=[
            pl.BlockSpec((SSD_BLOCK, GROUP_DIM), lambda g, sb: (sb, g)),
            pl.BlockSpec((seq_per_blk, None if aliased else DEPTH, 2, None, GROUP_DIM, D_STATE),
                         lambda g, sb: (jnp.minimum(sb, n_ctx_blk - 1), l if aliased else 0, 0, g, 0, 0)),
        ],
        out_shape=[
            jax.ShapeDtypeStruct((N_TOK, D_INNER), BF16),
            jax.ShapeDtypeStruct((BATCH, DEPTH, 2, N_GROUPS, GROUP_DIM, D_STATE), F32),
        ],
        scratch_shapes=[
            pltpu.VMEM((n_chunks, CHUNK, LANES), F32),
            pltpu.VMEM((n_chunks, 2 * HEADS_PER_GROUP, CHUNK), F32),
            pltpu.VMEM((n_chunks, 2 * HEADS_PER_GROUP, CHUNK), F32),
            pltpu.VMEM((n_chunks, 8, LANES), F32),
            pltpu.VMEM((n_chunks, D_STATE, CHUNK), F32),
            pltpu.VMEM((SSD_BLOCK, D_STATE), BF16),
            pltpu.VMEM((n_chunks, CHUNK, CHUNK), F32),
            pltpu.VMEM((n_chunks, GROUP_DIM // LANES, 2 * CHUNK, LANES), BF16),
            pltpu.VMEM((SSD_BLOCK, GROUP_DIM), F32),
            pltpu.VMEM((SSD_BLOCK, GROUP_DIM), F32),
            pltpu.VMEM((D_STATE, GROUP_DIM), F32),
            pltpu.VMEM((D_STATE, GROUP_DIM), F32),
        ],
        input_output_aliases={len(inputs) - 1: 1} if aliased else {},
        compiler_params=_params("arbitrary", "arbitrary"),
        name="ssd",
    )(*inputs)


def _final_norm_kernel(x_ref, w_ref, o_ref):
    x = x_ref[...]
    ms = jnp.mean(x * x, axis=-1, keepdims=True)
    o_ref[...] = x * lax.rsqrt(ms + EPS) * w_ref[...]


def _final_norm_call(x_all, w, row0, n_rows):
    tm = 512
    return pl.pallas_call(
        _final_norm_kernel,
        grid=(n_rows // tm,),
        in_specs=[
            pl.BlockSpec((tm, D_MODEL), lambda i: (row0 // tm + i, 0)),
            pl.BlockSpec((1, D_MODEL), lambda i: (0, 0)),
        ],
        out_specs=pl.BlockSpec((tm, D_MODEL), lambda i: (i, 0)),
        out_shape=jax.ShapeDtypeStruct((n_rows, D_MODEL), F32),
        compiler_params=_params("arbitrary"),
        name="final_norm",
    )(x_all, w.reshape(1, D_MODEL))


def _per_group_lanes(a):
    lead = a.shape[:-2]
    a = a.reshape(*lead, 2, N_GROUPS, HEADS_PER_GROUP)
    a = jnp.moveaxis(a, -3, -2).reshape(*lead, N_GROUPS, 2 * HEADS_PER_GROUP)
    a = jnp.pad(a, [(0, 0)] * (len(lead) + 1) + [(0, LANES - 2 * HEADS_PER_GROUP)])
    return a.reshape(*lead, N_GROUPS * LANES)


def kernel(x_prompt, x_sample, state_ssm, c, c_ctx, w_ada, b_ada, norm_w, w_in, conv_w, conv_b, a_log,
           dt_bias, d_skip, ssm_norm_w, w_out_ssm, sc_conv_w, w_out_sc, w_o, w_up, ffn_conv_w, w_down,
           final_norm_w):
    x_all = (x_prompt.reshape(N_CTX_TOK, D_MODEL), x_sample.reshape(N_LAT_TOK, D_MODEL))
    cvec = jnp.zeros((MOD_ROWS, D_MODEL), F32).at[0].set(c_ctx).at[1:1 + DEC_BATCH].set(c)
    mod = _mod_call(cvec, w_ada, b_ada)

    w_dt = _per_group_lanes(w_in[:, :, COL_DT:COL_SC].reshape(DEPTH, D_MODEL, 2, N_HEADS))
    dtb_x = _per_group_lanes(dt_bias).reshape(DEPTH, 1, N_GROUPS * LANES)
    alog_x = _per_group_lanes(a_log).reshape(DEPTH, 1, N_GROUPS * LANES)
    dsk = jnp.repeat(d_skip, HEAD_DIM, axis=-1).reshape(DEPTH, 1, D_INNER)
    conv_b3 = conv_b.reshape(DEPTH, 1, CONV_DIM)
    ssm_nw3 = ssm_norm_w.reshape(DEPTH, 1, D_INNER)

    tm = 1024
    states = None
    for l in range(DEPTH):
        def gate_of_mod(which, tn):
            return pl.BlockSpec((None, MOD_ROWS, tn), lambda j, i: (l, 0, (2 + 3 * which) * (D_MODEL // tn) + j))

        def residual(tile_m):
            def epilogue(acc, o_ref, j, i, *refs):
                g_ref = refs[-1]
                o_ref[...] = (_x_tile(refs[:-1], i, tile_m)
                              + g_ref[pl.ds(_group_of_tile(i, tile_m), 1), :] * acc)
            return epilogue

        h = _prenorm_call(x_all, norm_w, mod, l, 0)
        zs = _ws_matmul(h, w_in, l, 0, D_INNER, tm, 1024, F32, "inproj_z", epilogue=_silu_store)
        xbc = _ws_matmul(
            h, w_in, l, D_INNER, CONV_DIM, tm, 1024, F32, "inproj_xbc",
            extra=(conv_w, conv_b3),
            extra_specs=(pl.BlockSpec((None, 3, 1024), lambda j, i: (l, 0, j)),
                         pl.BlockSpec((None, 1, 1024), lambda j, i: (l, 0, j))),
            epilogue=functools.partial(_conv_silu_store, tm=tm))
        dt_raw = _ws_matmul(h, w_dt, l, 0, N_GROUPS * LANES, tm, 1024, F32, "inproj_dt")
        u_sc = _scu_call(h, w_in, sc_conv_w, l)
        sg = _ws_matmul(h, w_in, l, COL_GATES, 2 * D_MODEL, tm, 1024, F32, "inproj_gates", epilogue=_sigmoid_store)

        yn, states = _ssd_call(zs, xbc, dt_raw, dtb_x, alog_x, dsk, ssm_nw3, state_ssm, states, l)

        m = _branch_mix_call(yn, u_sc, w_out_ssm, w_out_sc, sg, l)
        xs, x_specs = _x_specs(x_all, tm, 1024, lambda j, i: (i, j))
        x_all = _ws_matmul(
            m, w_o, l, 0, D_MODEL, tm, 1024, F32, "merge_o",
            extra=(*xs, mod), extra_specs=(*x_specs, gate_of_mod(0, 1024)), epilogue=residual(tm))

        h2 = _prenorm_call(x_all, norm_w, mod, l, 1)
        act = (_ffn_up_call(h2, w_up, ffn_conv_w, l, False), _ffn_up_call(h2, w_up, ffn_conv_w, l, True))
        tmd = 512
        x_all = _ws_matmul(
            act, w_down, l, 0, D_MODEL, tmd, 512, F32, "ffn_down",
            extra=(x_all, mod),
            extra_specs=(pl.BlockSpec((tmd, 512), lambda j, i: (i, j)), gate_of_mod(1, 512)),
            epilogue=residual(tmd))

    y_prompt = _final_norm_call(x_all, final_norm_w, 0, N_CTX_TOK).reshape(BATCH, SEQ, D_MODEL)
    y_sample = _final_norm_call(x_all, final_norm_w, N_CTX_TOK, N_LAT_TOK).reshape(DEC_BATCH, DEC_SEQ, D_MODEL)
    new_state = states.reshape(BATCH, DEPTH, 2, N_HEADS, HEAD_DIM, D_STATE)
    return (y_prompt, y_sample, new_state)
```

```python
import functools

import jax
import jax.numpy as jnp
from jax import lax
from jax.experimental import pallas as pl
from jax.experimental.pallas import tpu as pltpu

F32 = jnp.float32
BF16 = jnp.bfloat16

D_MODEL = 2048
BATCH = 32
SEQ = 256
DEPTH = 2
DEC_BATCH = 2
DEC_SEQ = 2048
GRID_W = 64
D_INNER = 4096
HEAD_DIM = 64
N_HEADS = 64
N_GROUPS = 8
HEADS_PER_GROUP = N_HEADS // N_GROUPS
GROUP_DIM = D_INNER // N_GROUPS
D_STATE = 128
CHUNK = 128
CONV_DIM = D_INNER + 2 * N_GROUPS * D_STATE
D_FF = 5632
EPS = 1e-6
LOG2E = 1.4426950408889634
N_CTX_TOK = BATCH * SEQ
N_LAT_TOK = DEC_BATCH * DEC_SEQ
N_TOK = N_CTX_TOK + N_LAT_TOK
COL_DT = D_INNER + CONV_DIM
COL_SC = COL_DT + 2 * N_HEADS
COL_GATES = COL_SC + 3 * D_MODEL
MOD_ROWS = 16
SSD_BLOCK = 2048
FFN_ROWS = 512
LANES = 128
VMEM_LIMIT = 60 * 1024 * 1024


def _sigmoid(v):
    return 1.0 / (1.0 + jnp.exp(-v))


def _silu(v):
    return v * _sigmoid(v)


def _softplus(v):
    return jnp.maximum(v, 0.0) + jnp.log1p(jnp.exp(-jnp.abs(v)))


def _group_of_tile(i, tm):
    n_ctx = N_CTX_TOK // tm
    per_lat = DEC_SEQ // tm
    return jnp.where(i < n_ctx, 0, 1 + lax.div(jnp.maximum(i - n_ctx, 0), per_lat))


def _row_conv3(x, w, last):
    n = x.shape[0]
    pos = lax.broadcasted_iota(jnp.int32, x.shape, 0) & last
    prev = jnp.where(pos == 0, 0.0, pltpu.roll(x, 1, 0))
    nxt = jnp.where(pos == last, 0.0, pltpu.roll(x, n - 1, 0))
    return prev * w[0:1] + x * w[1:2] + nxt * w[2:3]


def _params(*sem):
    return pltpu.CompilerParams(dimension_semantics=sem, vmem_limit_bytes=VMEM_LIMIT)


def _mod_kernel(c_ref, w_ref, b_ref, o_ref):
    a = _silu(c_ref[...]).astype(BF16)
    o_ref[...] = jnp.dot(a, w_ref[...].astype(BF16), preferred_element_type=F32) + b_ref[...]


def _mod_call(cvec, w_ada, b_ada):
    tn = 1024
    n = 6 * D_MODEL
    return pl.pallas_call(
        _mod_kernel,
        grid=(DEPTH, n // tn),
        in_specs=[
            pl.BlockSpec((MOD_ROWS, D_MODEL), lambda l, j: (0, 0)),
            pl.BlockSpec((None, D_MODEL, tn), lambda l, j: (l, 0, j)),
            pl.BlockSpec((None, 1, tn), lambda l, j: (l, 0, j)),
        ],
        out_specs=pl.BlockSpec((None, MOD_ROWS, tn), lambda l, j: (l, 0, j)),
        out_shape=jax.ShapeDtypeStruct((DEPTH, MOD_ROWS, n), F32),
        compiler_params=_params("arbitrary", "arbitrary"),
        name="adaln_mod",
    )(cvec, w_ada, b_ada.reshape(DEPTH, 1, n))


def _x_specs(x, tm, tn, index):
    if not isinstance(x, tuple):
        return (x,), (pl.BlockSpec((tm, tn), index),)
    n_ctx = N_CTX_TOK // tm

    def ctx_index(*grid):
        r, c = index(*grid)
        return jnp.minimum(r, n_ctx - 1), c

    def lat_index(*grid):
        r, c = index(*grid)
        return jnp.maximum(r - n_ctx, 0), c
    return x, (pl.BlockSpec((tm, tn), ctx_index), pl.BlockSpec((tm, tn), lat_index))


def _x_tile(x_refs, i, tm):
    if len(x_refs) == 1:
        return x_refs[0][...]
    return jnp.where(i < N_CTX_TOK // tm, x_refs[0][...], x_refs[1][...])


def _prenorm_kernel(*refs, tm, n_x):
    nw_ref, sh_ref, sc_ref, o_ref = refs[n_x:]
    i = pl.program_id(0)
    g = _group_of_tile(i, tm)
    x = _x_tile(refs[:n_x], i, tm)
    ms = jnp.mean(x * x, axis=-1, keepdims=True)
    y = x * lax.rsqrt(ms + EPS) * nw_ref[...]
    o_ref[...] = (y * (1.0 + sc_ref[pl.ds(g, 1), :]) + sh_ref[pl.ds(g, 1), :]).astype(BF16)


def _prenorm_call(x, norm_w, mod, l, which):
    tm = 512
    xs, x_specs = _x_specs(x, tm, D_MODEL, lambda i: (i, 0))
    return pl.pallas_call(
        functools.partial(_prenorm_kernel, tm=tm, n_x=len(xs)),
        grid=(N_TOK // tm,),
        in_specs=[
            *x_specs,
            pl.BlockSpec((None, 1, D_MODEL), lambda i: (2 * l + which, 0, 0)),
            pl.BlockSpec((None, MOD_ROWS, D_MODEL), lambda i: (l, 0, 3 * which)),
            pl.BlockSpec((None, MOD_ROWS, D_MODEL), lambda i: (l, 0, 3 * which + 1)),
        ],
        out_specs=pl.BlockSpec((tm, D_MODEL), lambda i: (i, 0)),
        out_shape=jax.ShapeDtypeStruct((N_TOK, D_MODEL), BF16),
        compiler_params=_params("arbitrary"),
        name=f"prenorm{which}",
    )(*xs, norm_w.reshape(DEPTH * 2, 1, D_MODEL), mod, mod)


def _w_spec(k, tn, l, col_off):
    if col_off % tn == 0:
        return pl.BlockSpec((None, k, tn), lambda j, i: (l, 0, col_off // tn + j))
    return pl.BlockSpec((pl.Element(1), pl.Element(k), pl.Element(tn)),
                        lambda j, i: (l, 0, pl.multiple_of(col_off + j * tn, LANES)))


def _cast_weight(w_ref, wb_ref):
    w = w_ref[0] if len(w_ref.shape) == 3 else w_ref[...]
    wb_ref[...] = w.astype(BF16)


def _mm_kernel(*refs, n_h, n_extra, epilogue, tm):
    w_ref = refs[n_h]
    extra = refs[n_h + 1:n_h + 1 + n_extra]
    o_ref, wb_ref = refs[n_h + 1 + n_extra:]
    j, i = pl.program_id(0), pl.program_id(1)

    @pl.when(i == 0)
    def _():
        _cast_weight(w_ref, wb_ref)

    acc = jnp.dot(_x_tile(refs[:n_h], i, tm), wb_ref[...], preferred_element_type=F32)
    epilogue(acc, o_ref, j, i, *extra)


def _store(acc, o_ref, j, i):
    o_ref[...] = acc.astype(o_ref.dtype)


def _ws_matmul(h, w, l, col_off, n_cols, tm, tn, out_dtype, name, extra=(), extra_specs=(), epilogue=_store):
    k = h[0].shape[1] if isinstance(h, tuple) else h.shape[1]
    hs, h_specs = _x_specs(h, tm, k, lambda j, i: (i, 0))
    return pl.pallas_call(
        functools.partial(_mm_kernel, n_h=len(hs), n_extra=len(extra), epilogue=epilogue, tm=tm),
        grid=(n_cols // tn, N_TOK // tm),
        in_specs=[*h_specs, _w_spec(k, tn, l, col_off), *extra_specs],
        out_specs=pl.BlockSpec((tm, tn), lambda j, i: (i, j)),
        out_shape=jax.ShapeDtypeStruct((N_TOK, n_cols), out_dtype),
        scratch_shapes=[pltpu.VMEM((k, tn), BF16)],
        compiler_params=_params("arbitrary", "arbitrary"),
        name=name,
    )(*hs, w, *extra)


def _silu_store(acc, o_ref, j, i):
    o_ref[...] = _silu(acc)


def _sigmoid_store(acc, o_ref, j, i):
    o_ref[...] = _sigmoid(acc)


def _conv_silu_store(acc, o_ref, j, i, cw_ref, cb_ref, *, tm):
    last = jnp.where(i < N_CTX_TOK // tm, SEQ - 1, GRID_W - 1)
    o_ref[...] = _silu(_row_conv3(acc, cw_ref[...], last) + cb_ref[...])


def _scu_kernel(h_ref, wb_ref, wc_ref, wh_ref, cw_ref, o_ref, wb_s, wc_s, wh_s, *, tm):
    i = pl.program_id(1)

    @pl.when(i == 0)
    def _():
        _cast_weight(wb_ref, wb_s)
        _cast_weight(wc_ref, wc_s)
        _cast_weight(wh_ref, wh_s)

    last = jnp.where(i < N_CTX_TOK // tm, SEQ - 1, GRID_W - 1)
    h = h_ref[...]
    v = jnp.dot(h, wc_s[...], preferred_element_type=F32) * jnp.dot(h, wh_s[...], preferred_element_type=F32)
    o_ref[...] = (jnp.dot(h, wb_s[...], preferred_element_type=F32) * _row_conv3(v, cw_ref[...], last)).astype(BF16)


def _scu_call(h, w_in, sc_conv_w, l):
    tm, tn = 1024, 512
    return pl.pallas_call(
        functools.partial(_scu_kernel, tm=tm),
        grid=(D_MODEL // tn, N_TOK // tm),
        in_specs=[
            pl.BlockSpec((tm, D_MODEL), lambda j, i: (i, 0)),
            _w_spec(D_MODEL, tn, l, COL_SC),
            _w_spec(D_MODEL, tn, l, COL_SC + D_MODEL),
            _w_spec(D_MODEL, tn, l, COL_SC + 2 * D_MODEL),
            pl.BlockSpec((None, 3, tn), lambda j, i: (l, 0, j)),
        ],
        out_specs=pl.BlockSpec((tm, tn), lambda j, i: (i, j)),
        out_shape=jax.ShapeDtypeStruct((N_TOK, D_MODEL), BF16),
        scratch_shapes=[pltpu.VMEM((D_MODEL, tn), BF16)] * 3,
        compiler_params=_params("arbitrary", "arbitrary"),
        name="inproj_scu",
    )(h, w_in, w_in, w_in, sc_conv_w)


def _branch_mix_kernel(yn_ref, u_ref, wa_ref, wb_ref, ga_ref, gb_ref, o_ref, wa_s, wb_s):
    @pl.when(pl.program_id(1) == 0)
    def _():
        _cast_weight(wa_ref, wa_s)
        _cast_weight(wb_ref, wb_s)

    y_ssm = jnp.dot(yn_ref[...], wa_s[...], preferred_element_type=F32)
    y_sc = jnp.dot(u_ref[...], wb_s[...], preferred_element_type=F32)
    o_ref[...] = (ga_ref[...] * y_ssm + gb_ref[...] * y_sc).astype(BF16)


def _branch_mix_call(yn, u_sc, w_out_ssm, w_out_sc, sg, l):
    tm, tn = 512, 512
    nb = D_MODEL // tn
    return pl.pallas_call(
        _branch_mix_kernel,
        grid=(nb, N_TOK // tm),
        in_specs=[
            pl.BlockSpec((tm, D_INNER), lambda j, i: (i, 0)),
            pl.BlockSpec((tm, D_MODEL), lambda j, i: (i, 0)),
            _w_spec(D_INNER, tn, l, 0),
            _w_spec(D_MODEL, tn, l, 0),
            pl.BlockSpec((tm, tn), lambda j, i: (i, j)),
            pl.BlockSpec((tm, tn), lambda j, i: (i, nb + j)),
        ],
        out_specs=pl.BlockSpec((tm, tn), lambda j, i: (i, j)),
        out_shape=jax.ShapeDtypeStruct((N_TOK, D_MODEL), BF16),
        scratch_shapes=[pltpu.VMEM((D_INNER, tn), BF16), pltpu.VMEM((D_MODEL, tn), BF16)],
        compiler_params=_params("arbitrary", "arbitrary"),
        name="branch_mix",
    )(yn, u_sc, w_out_ssm, w_out_sc, sg, sg)


def _merge_norm_kernel(*refs, tm, n_x):
    m_ref, w_ref = refs[:2]
    x_refs = refs[2:2 + n_x]
    mod_ref, nw_ref, x1_ref, h_ref, wb_s = refs[2 + n_x:]
    i = pl.program_id(0)

    @pl.when(i == 0)
    def _():
        _cast_weight(w_ref, wb_s)

    mod = mod_ref[pl.ds(_group_of_tile(i, tm), 1), :]
    part = lambda k: mod[:, k * D_MODEL:(k + 1) * D_MODEL]
    acc = jnp.dot(m_ref[...], wb_s[...], preferred_element_type=F32)
    x1 = _x_tile(x_refs, i, tm) + part(2) * acc
    x1_ref[...] = x1
    ms = jnp.mean(x1 * x1, axis=-1, keepdims=True)
    y = x1 * lax.rsqrt(ms + EPS) * nw_ref[...]
    h_ref[...] = (y * (1.0 + part(4)) + part(3)).astype(BF16)


def _merge_norm_call(m, w_o, x, mod, norm_w, l):
    tm = 256 if isinstance(x, tuple) else 512
    xs, x_specs = _x_specs(x, tm, D_MODEL, lambda i: (i, 0))
    row_tile = pl.BlockSpec((tm, D_MODEL), lambda i: (i, 0))
    return pl.pallas_call(
        functools.partial(_merge_norm_kernel, tm=tm, n_x=len(xs)),
        grid=(N_TOK // tm,),
        in_specs=[
            row_tile,
            pl.BlockSpec((None, D_MODEL, D_MODEL), lambda i: (l, 0, 0), pipeline_mode=pl.Buffered(1)),
            *x_specs,
            pl.BlockSpec((None, MOD_ROWS, 6 * D_MODEL), lambda i: (l, 0, 0)),
            pl.BlockSpec((None, 1, D_MODEL), lambda i: (2 * l + 1, 0, 0)),
        ],
        out_specs=[row_tile, row_tile],
        out_shape=[jax.ShapeDtypeStruct((N_TOK, D_MODEL), F32), jax.ShapeDtypeStruct((N_TOK, D_MODEL), BF16)],
        scratch_shapes=[pltpu.VMEM((D_MODEL, D_MODEL), BF16)],
        compiler_params=_params("arbitrary"),
        name="merge_norm",
    )(m, w_o, *xs, mod, norm_w.reshape(DEPTH * 2, 1, D_MODEL))


def _ffn_up_kernel(h_ref, wg_ref, wv_ref, cg_ref, cv_ref, o_ref, wg_s, wv_s, g_s, v_s, *, row_len, vertical):
    tm = h_ref.shape[0]
    halo = row_len if vertical else 0

    @pl.when(pl.program_id(1) == 0)
    def _():
        _cast_weight(wg_ref, wg_s)
        _cast_weight(wv_ref, wv_s)
        if halo:
            for s in (g_s, v_s):
                s[0:halo, :] = jnp.zeros((halo, s.shape[1]), F32)
                s[halo + tm:, :] = jnp.zeros((halo, s.shape[1]), F32)

    def conv(s_ref, w, r0):
        ext = s_ref[r0:r0 + FFN_ROWS + 2 * halo, :]
        n = ext.shape[0]
        pos = lax.broadcasted_iota(jnp.int32, ext.shape, 0) & (row_len - 1)
        xl = jnp.where(pos == 0, 0.0, pltpu.roll(ext, 1, 0))
        xr = jnp.where(pos == row_len - 1, 0.0, pltpu.roll(ext, n - 1, 0))
        y = None
        for a in ((0, 1, 2) if vertical else (1,)):
            sl = slice(halo + (a - 1) * row_len, halo + (a - 1) * row_len + FFN_ROWS) if vertical else slice(0, n)
            term = xl[sl] * w[3 * a:3 * a + 1] + ext[sl] * w[3 * a + 1:3 * a + 2] + xr[sl] * w[3 * a + 2:3 * a + 3]
            y = term if y is None else y + term
        return y

    cg, cv = cg_ref[...], cv_ref[...]
    n_blocks = tm // FFN_ROWS
    for b in range(n_blocks + 1):
        if b < n_blocks:
            hb = h_ref[b * FFN_ROWS:(b + 1) * FFN_ROWS, :]
            rows = slice(halo + b * FFN_ROWS, halo + (b + 1) * FFN_ROWS)
            g_s[rows, :] = jnp.dot(hb, wg_s[...], preferred_element_type=F32)
            v_s[rows, :] = jnp.dot(hb, wv_s[...], preferred_element_type=F32)
        if b >= 1:
            r0 = (b - 1) * FFN_ROWS
            o_ref[r0:r0 + FFN_ROWS, :] = (_silu(conv(g_s, cg, r0)) * conv(v_s, cv, r0)).astype(BF16)


def _ffn_up_call(h, w_up, ffn_conv_w, l, latent):
    tm, tn = DEC_SEQ, 512
    nb = D_FF // tn
    w9 = ffn_conv_w.reshape(DEPTH, 9, 2 * D_FF)
    tile0, n_rows = (N_CTX_TOK // tm, N_LAT_TOK) if latent else (0, N_CTX_TOK)
    return pl.pallas_call(
        functools.partial(_ffn_up_kernel, row_len=GRID_W if latent else SEQ, vertical=latent),
        grid=(nb, n_rows // tm),
        in_specs=[
            pl.BlockSpec((tm, D_MODEL), lambda j, i: (tile0 + i, 0)),
            _w_spec(D_MODEL, tn, l, 0),
            _w_spec(D_MODEL, tn, l, D_FF),
            pl.BlockSpec((None, 9, tn), lambda j, i: (l, 0, j)),
            pl.BlockSpec((None, 9, tn), lambda j, i: (l, 0, nb + j)),
        ],
        out_specs=pl.BlockSpec((tm, tn), lambda j, i: (i, j)),
        out_shape=jax.ShapeDtypeStruct((n_rows, D_FF), BF16),
        scratch_shapes=[pltpu.VMEM((D_MODEL, tn), BF16)] * 2
        + [pltpu.VMEM((tm + (2 * GRID_W if latent else 0), tn), F32)] * 2,
        compiler_params=_params("arbitrary", "arbitrary"),
        name="ffn_up_lat" if latent else "ffn_up_ctx",
    )(h, w_up, w_up, w9, w9)


def _prefix_sum_rows(a, qi):
    s = 1
    while s < CHUNK:
        a = a + jnp.where(qi >= s, pltpu.roll(a, s, 0), 0.0)
        s *= 2
    return a


def _lane_bcast(a, j):
    return jnp.broadcast_to(a[:, j:j + 1], (a.shape[0], LANES))


def _ssd_kernel(*refs, aliased, layer):
    (zs_ref, xs_ref, b_ref, c_ref, dt_ref, dtb_ref, al_ref, dsk_ref, nw_ref, h0_ref) = refs[:10]
    (y_ref, st_ref, cum_s, rt_s, wt_s, tot_s, bt_s, cm_s, cb_s, rhs_s,
     y_s, stf_s, stb_s) = refs[10 + int(aliased):]
    sb = pl.program_id(1)
    dt_shift = lax.rem(LANES - 2 * HEADS_PER_GROUP * pl.program_id(0), LANES)
    a2_row = -jnp.exp(al_ref[...]) * LOG2E
    qi = lax.broadcasted_iota(jnp.int32, (CHUNK, CHUNK), 0)
    ki = lax.broadcasted_iota(jnp.int32, (CHUNK, CHUNK), 1)
    lo = ki < HEAD_DIM
    lo_row = lo[0:1]
    n_lanes_used = 2 * HEADS_PER_GROUP
    fwd_lane = ki < HEADS_PER_GROUP
    fwd_row = lax.broadcasted_iota(jnp.int32, (n_lanes_used, 1), 0) < HEADS_PER_GROUP

    def prep(p, carry):
        rows = pl.ds(pl.multiple_of(p * CHUNK, CHUNK), CHUNK)
        dtv = _softplus(pltpu.roll(dt_ref[rows, :], dt_shift, 1) + dtb_ref[...])
        a = dtv * a2_row
        pre = _prefix_sum_rows(a, qi)
        tot = pre[CHUNK - 1:CHUNK, :]
        cum = jnp.where(fwd_lane, pre, tot - pre + a)
        cum_t = cum.T[0:n_lanes_used]
        dt_t = dtv.T[0:n_lanes_used]
        tot_t = jnp.where(fwd_row, cum_t[:, CHUNK - 1:CHUNK], cum_t[:, 0:1])
        cum_s[p] = cum
        tot_s[p] = jnp.broadcast_to(tot, tot_s.shape[1:])
        rt_s[p] = cum_t - jnp.log2(dt_t)
        wt_s[p] = jnp.exp2(tot_t - cum_t) * dt_t
        bt = b_ref[rows, :].T
        cm = c_ref[rows, :].astype(BF16)
        bt_s[p] = bt
        cm_s[rows, :] = cm
        cb_s[p] = jnp.dot(cm, bt.astype(BF16), preferred_element_type=F32)
        for t in range(GROUP_DIM // LANES):
            xt = xs_ref[rows, t * LANES:(t + 1) * LANES]
            rhs_s[p, t] = jnp.concatenate([jnp.where(lo, xt, 0.0), jnp.where(lo, 0.0, xt)], axis=0).astype(BF16)
        return carry
    lax.fori_loop(0, SSD_BLOCK // CHUNK, prep, 0, unroll=4)

    def chunk_head(c, d, st_s, zero_state):
        rows = pl.ds(pl.multiple_of(c * CHUNK, CHUNK), CHUNK)
        p = dict(c=c, rows=rows, cum=cum_s[c], tot=tot_s[c, 0:1], rt=rt_s[c], wt=wt_s[c], bt=bt_s[c],
                 cb=cb_s[c], tri=(qi >= ki) if d == 0 else (qi <= ki))
        if not zero_state:
            p["s_in"] = st_s[...]
            p["y_off"] = jnp.dot(cm_s[rows, :], p["s_in"].astype(BF16), preferred_element_type=F32)
        return p

    def head_pair(t, d, p, st_s, first_visit):
        tile = slice(t * LANES, (t + 1) * LANES)
        ha, hb = HEADS_PER_GROUP * d + 2 * t, HEADS_PER_GROUP * d + 2 * t + 1
        cols, tops, bots = [], [], []
        for hl in (ha, hb):
            col = _lane_bcast(p["cum"], hl)
            decay_dt = jnp.exp2(jnp.where(p["tri"], col - p["rt"][hl:hl + 1, :], -jnp.inf))
            cols.append(col)
            tops.append((p["cb"] * decay_dt).astype(BF16))
            bots.append((p["bt"] * p["wt"][hl:hl + 1, :]).astype(BF16))
        lhs = jnp.concatenate([jnp.concatenate(tops, axis=1), jnp.concatenate(bots, axis=1)], axis=0)
        out = jnp.dot(lhs, rhs_s[p["c"], t], preferred_element_type=F32)
        y, s_new = out[0:CHUNK], out[CHUNK:2 * CHUNK]
        if "s_in" in p:
            e_in = jnp.exp2(jnp.where(lo, cols[0], cols[1]))
            y = y + p["y_off"][:, tile] * e_in
            dec = jnp.exp2(jnp.where(lo_row, _lane_bcast(p["tot"], ha), _lane_bcast(p["tot"], hb)))
            s_new = s_new + p["s_in"][:, tile] * dec
        st_s[:, tile] = s_new
        y_s[p["rows"], tile] = y if first_visit else y_s[p["rows"], tile] + y

    def finish_rows(rows):
        y = y_s[rows, :] + dsk_ref[...] * xs_ref[rows, :]
        yz = y * zs_ref[rows, :]
        msq = jnp.mean(yz * yz, axis=-1, keepdims=True)
        y_ref[rows, :] = (yz * lax.rsqrt(msq + EPS) * nw_ref[...]).astype(BF16)

    def chunk_both(c_fwd, c_bwd, first_visit, zero_state=False):
        pf = chunk_head(c_fwd, 0, stf_s, zero_state)
        pb = chunk_head(c_bwd, 1, stb_s, zero_state)
        for t in range(GROUP_DIM // LANES):
            head_pair(t, 0, pf, stf_s, first_visit)
            head_pair(t, 1, pb, stb_s, first_visit)
        if not first_visit:
            finish_rows(pf["rows"])
            finish_rows(pb["rows"])

    @pl.when(sb < N_CTX_TOK // SSD_BLOCK)
    def _context():
        nc = SEQ // CHUNK

        def seq_body(s, carry):
            chunk_both(s * nc, s * nc + 1, first_visit=True, zero_state=True)
            chunk_both(s * nc + 1, s * nc, first_visit=False)
            for d, st_s in enumerate((stf_s, stb_s)):
                if aliased:
                    st_ref[s, d] = st_s[...].T
                else:
                    for other in range(DEPTH):
                        st_ref[s, other, d] = st_s[...].T if other == layer else jnp.zeros(st_ref.shape[3:], F32)
            return carry
        lax.fori_loop(0, SSD_BLOCK // SEQ, seq_body, 0)

    @pl.when(sb >= N_CTX_TOK // SSD_BLOCK)
    def _latent():
        nc = DEC_SEQ // CHUNK
        stf_s[...] = h0_ref[0].T
        stb_s[...] = h0_ref[1].T

        def first_half(k, carry):
            chunk_both(k, nc - 1 - k, first_visit=True)
            return carry
        lax.fori_loop(0, nc // 2, first_half, 0)

        def second_half(k, carry):
            chunk_both(k, nc - 1 - k, first_visit=False)
            return carry
        lax.fori_loop(nc // 2, nc, second_half, 0)


def _ssd_call(zs, xbc, dt_raw, dtb_x, alog_x, dsk, ssm_norm_w, state_ssm, prev_states, l):
    n_ctx_blk = N_CTX_TOK // SSD_BLOCK
    n_blk = N_TOK // SSD_BLOCK
    seq_per_blk = SSD_BLOCK // SEQ
    n_chunks = SSD_BLOCK // CHUNK
    b_cb = D_INNER // D_STATE
    c_cb = b_cb + N_GROUPS
    rowvec = lambda width: pl.BlockSpec((None, 1, width), lambda g, sb: (l, 0, g))
    h0 = state_ssm.reshape(DEC_BATCH, DEPTH, 2, N_GROUPS, GROUP_DIM, D_STATE)
    aliased = prev_states is not None
    inputs = [zs, xbc, xbc, xbc, dt_raw, dtb_x, alog_x, dsk, ssm_norm_w, h0]
    in_specs = [
        pl.BlockSpec((SSD_BLOCK, GROUP_DIM), lambda g, sb: (sb, g)),
        pl.BlockSpec((SSD_BLOCK, GROUP_DIM), lambda g, sb: (sb, g)),
        pl.BlockSpec((SSD_BLOCK, D_STATE), lambda g, sb: (sb, b_cb + g)),
        pl.BlockSpec((SSD_BLOCK, D_STATE), lambda g, sb: (sb, c_cb + g)),
        pl.BlockSpec((SSD_BLOCK, LANES), lambda g, sb: (sb, 0)),
        rowvec(LANES), rowvec(LANES), rowvec(GROUP_DIM), rowvec(GROUP_DIM),
        pl.BlockSpec((None, None, 2, None, GROUP_DIM, D_STATE),
                     lambda g, sb: (jnp.maximum(sb - n_ctx_blk, 0), l, 0, g, 0, 0)),
    ]
    if aliased:
        inputs.append(prev_states)
        in_specs.append(pl.BlockSpec(memory_space=pl.ANY))
    return pl.pallas_call(
        functools.partial(_ssd_kernel, aliased=aliased, layer=l),
        grid=(N_GROUPS, n_blk),
        in_specs=in_specs,
        out_specs=[
            pl.BlockSpec((SSD_BLOCK, GROUP_DIM), lambda g, sb: (sb, g)),
            pl.BlockSpec((seq_per_blk, None if aliased else DEPTH, 2, None, GROUP_DIM, D_STATE),
                         lambda g, sb: (jnp.minimum(sb, n_ctx_blk - 1), l if aliased else 0, 0, g, 0, 0)),
        ],
        out_shape=[
            jax.ShapeDtypeStruct((N_TOK, D_INNER), BF16),
            jax.ShapeDtypeStruct((BATCH, DEPTH, 2, N_GROUPS, GROUP_DIM, D_STATE), F32),
        ],
        scratch_shapes=[
            pltpu.VMEM((n_chunks, CHUNK, LANES), F32),
            pltpu.VMEM((n_chunks, 2 * HEADS_PER_GROUP, CHUNK), F32),
            pltpu.VMEM((n_chunks, 2 * HEADS_PER_GROUP, CHUNK), F32),
            pltpu.VMEM((n_chunks, 8, LANES), F32),
            pltpu.VMEM((n_chunks, D_STATE, CHUNK), F32),
            pltpu.VMEM((SSD_BLOCK, D_STATE), BF16),
            pltpu.VMEM((n_chunks, CHUNK, CHUNK), F32),
            pltpu.VMEM((n_chunks, GROUP_DIM // LANES, 2 * CHUNK, LANES), BF16),
            pltpu.VMEM((SSD_BLOCK, GROUP_DIM), F32),
            pltpu.VMEM((D_STATE, GROUP_DIM), F32),
            pltpu.VMEM((D_STATE, GROUP_DIM), F32),
        ],
        input_output_aliases={len(inputs) - 1: 1} if aliased else {},
        compiler_params=_params("arbitrary", "arbitrary"),
        name="ssd",
    )(*inputs)


def _final_norm_kernel(x_ref, w_ref, o_ref):
    x = x_ref[...]
    ms = jnp.mean(x * x, axis=-1, keepdims=True)
    o_ref[...] = x * lax.rsqrt(ms + EPS) * w_ref[...]


def _final_norm_call(x_all, w, row0, n_rows):
    tm = 512
    return pl.pallas_call(
        _final_norm_kernel,
        grid=(n_rows // tm,),
        in_specs=[
            pl.BlockSpec((tm, D_MODEL), lambda i: (row0 // tm + i, 0)),
            pl.BlockSpec((1, D_MODEL), lambda i: (0, 0)),
        ],
        out_specs=pl.BlockSpec((tm, D_MODEL), lambda i: (i, 0)),
        out_shape=jax.ShapeDtypeStruct((n_rows, D_MODEL), F32),
        compiler_params=_params("arbitrary"),
        name="final_norm",
    )(x_all, w.reshape(1, D_MODEL))


def _per_group_lanes(a, pad):
    lead = a.shape[:-2]
    a = a.reshape(*lead, 2, N_GROUPS, HEADS_PER_GROUP)
    a = jnp.moveaxis(a, -3, -2).reshape(*lead, N_GROUPS, 2 * HEADS_PER_GROUP)
    if pad:
        a = jnp.pad(a, [(0, 0)] * (len(lead) + 1) + [(0, LANES - 2 * HEADS_PER_GROUP)])
    return a.reshape(*lead, -1)


def kernel(x_prompt, x_sample, state_ssm, c, c_ctx, w_ada, b_ada, norm_w, w_in, conv_w, conv_b, a_log,
           dt_bias, d_skip, ssm_norm_w, w_out_ssm, sc_conv_w, w_out_sc, w_o, w_up, ffn_conv_w, w_down,
           final_norm_w):
    x_all = (x_prompt.reshape(N_CTX_TOK, D_MODEL), x_sample.reshape(N_LAT_TOK, D_MODEL))
    cvec = jnp.zeros((MOD_ROWS, D_MODEL), F32).at[0].set(c_ctx).at[1:1 + DEC_BATCH].set(c)
    mod = _mod_call(cvec, w_ada, b_ada)

    w_dt = _per_group_lanes(w_in[:, :, COL_DT:COL_SC].reshape(DEPTH, D_MODEL, 2, N_HEADS), pad=False)
    dtb_x = _per_group_lanes(dt_bias, pad=True).reshape(DEPTH, 1, N_GROUPS * LANES)
    alog_x = _per_group_lanes(a_log, pad=True).reshape(DEPTH, 1, N_GROUPS * LANES)
    dsk = jnp.repeat(d_skip, HEAD_DIM, axis=-1).reshape(DEPTH, 1, D_INNER)
    conv_b3 = conv_b.reshape(DEPTH, 1, CONV_DIM)
    ssm_nw3 = ssm_norm_w.reshape(DEPTH, 1, D_INNER)

    tm = 1024
    states = None
    for l in range(DEPTH):
        def gate_of_mod(which, tn):
            return pl.BlockSpec((None, MOD_ROWS, tn), lambda j, i: (l, 0, (2 + 3 * which) * (D_MODEL // tn) + j))

        def residual(tile_m):
            def epilogue(acc, o_ref, j, i, *refs):
                g_ref = refs[-1]
                o_ref[...] = (_x_tile(refs[:-1], i, tile_m)
                              + g_ref[pl.ds(_group_of_tile(i, tile_m), 1), :] * acc)
            return epilogue

        h = _prenorm_call(x_all, norm_w, mod, l, 0)
        zs = _ws_matmul(h, w_in, l, 0, D_INNER, tm, 1024, F32, "inproj_z", epilogue=_silu_store)
        xbc = _ws_matmul(
            h, w_in, l, D_INNER, CONV_DIM, tm, 1024, F32, "inproj_xbc",
            extra=(conv_w, conv_b3),
            extra_specs=(pl.BlockSpec((None, 3, 1024), lambda j, i: (l, 0, j)),
                         pl.BlockSpec((None, 1, 1024), lambda j, i: (l, 0, j))),
            epilogue=functools.partial(_conv_silu_store, tm=tm))
        dt_raw = _ws_matmul(h, w_dt, l, 0, LANES, 2 * tm, LANES, F32, "inproj_dt")
        u_sc = _scu_call(h, w_in, sc_conv_w, l)
        sg = _ws_matmul(h, w_in, l, COL_GATES, 2 * D_MODEL, tm, 1024, F32, "inproj_gates", epilogue=_sigmoid_store)

        yn, states = _ssd_call(zs, xbc, dt_raw, dtb_x, alog_x, dsk, ssm_nw3, state_ssm, states, l)

        m = _branch_mix_call(yn, u_sc, w_out_ssm, w_out_sc, sg, l)
        x_all, h2 = _merge_norm_call(m, w_o, x_all, mod, norm_w, l)
        act = (_ffn_up_call(h2, w_up, ffn_conv_w, l, False), _ffn_up_call(h2, w_up, ffn_conv_w, l, True))
        tmd = 512
        x_all = _ws_matmul(
            act, w_down, l, 0, D_MODEL, tmd, 512, F32, "ffn_down",
            extra=(x_all, mod),
            extra_specs=(pl.BlockSpec((tmd, 512), lambda j, i: (i, j)), gate_of_mod(1, 512)),
            epilogue=residual(tmd))

    y_prompt = _final_norm_call(x_all, final_norm_w, 0, N_CTX_TOK).reshape(BATCH, SEQ, D_MODEL)
    y_sample = _final_norm_call(x_all, final_norm_w, N_CTX_TOK, N_LAT_TOK).reshape(DEC_BATCH, DEC_SEQ, D_MODEL)
    new_state = states.reshape(BATCH, DEPTH, 2, N_HEADS, HEAD_DIM, D_STATE)
    return (y_prompt, y_sample, new_state)
```

```python
import functools

import jax
import jax.numpy as jnp
from jax import lax
from jax.experimental import pallas as pl
from jax.experimental.pallas import tpu as pltpu

F32 = jnp.float32
BF16 = jnp.bfloat16

D_MODEL = 2048
BATCH = 32
SEQ = 256
DEPTH = 2
DEC_BATCH = 2
DEC_SEQ = 2048
GRID_W = 64
D_INNER = 4096
HEAD_DIM = 64
N_HEADS = 64
N_GROUPS = 8
HEADS_PER_GROUP = N_HEADS // N_GROUPS
GROUP_DIM = D_INNER // N_GROUPS
D_STATE = 128
CHUNK = 128
CONV_DIM = D_INNER + 2 * N_GROUPS * D_STATE
D_FF = 5632
EPS = 1e-6
LOG2E = 1.4426950408889634
N_CTX_TOK = BATCH * SEQ
N_LAT_TOK = DEC_BATCH * DEC_SEQ
N_TOK = N_CTX_TOK + N_LAT_TOK
COL_DT = D_INNER + CONV_DIM
COL_SC = COL_DT + 2 * N_HEADS
COL_GATES = COL_SC + 3 * D_MODEL
MOD_ROWS = 16
SSD_BLOCK = 2048
FFN_ROWS = 1024
LANES = 128
VMEM_LIMIT = 60 * 1024 * 1024


def _sigmoid(v):
    return 1.0 / (1.0 + jnp.exp(-v))


def _silu(v):
    return v * _sigmoid(v)


def _softplus(v):
    return jnp.maximum(v, 0.0) + jnp.log1p(jnp.exp(-jnp.abs(v)))


def _group_of_tile(i, tm):
    n_ctx = N_CTX_TOK // tm
    per_lat = DEC_SEQ // tm
    return jnp.where(i < n_ctx, 0, 1 + lax.div(jnp.maximum(i - n_ctx, 0), per_lat))


def _row_conv3(x, w, last):
    n = x.shape[0]
    pos = lax.broadcasted_iota(jnp.int32, x.shape, 0) & last
    prev = jnp.where(pos == 0, 0.0, pltpu.roll(x, 1, 0))
    nxt = jnp.where(pos == last, 0.0, pltpu.roll(x, n - 1, 0))
    return prev * w[0:1] + x * w[1:2] + nxt * w[2:3]


def _params(*sem):
    return pltpu.CompilerParams(dimension_semantics=sem, vmem_limit_bytes=VMEM_LIMIT)


def _mod_kernel(c_ref, w_ref, b_ref, o_ref):
    a = _silu(c_ref[...]).astype(BF16)
    o_ref[...] = jnp.dot(a, w_ref[...].astype(BF16), preferred_element_type=F32) + b_ref[...]


def _mod_call(cvec, w_ada, b_ada):
    tn = 1024
    n = 6 * D_MODEL
    return pl.pallas_call(
        _mod_kernel,
        grid=(DEPTH, n // tn),
        in_specs=[
            pl.BlockSpec((MOD_ROWS, D_MODEL), lambda l, j: (0, 0)),
            pl.BlockSpec((None, D_MODEL, tn), lambda l, j: (l, 0, j)),
            pl.BlockSpec((None, 1, tn), lambda l, j: (l, 0, j)),
        ],
        out_specs=pl.BlockSpec((None, MOD_ROWS, tn), lambda l, j: (l, 0, j)),
        out_shape=jax.ShapeDtypeStruct((DEPTH, MOD_ROWS, n), F32),
        compiler_params=_params("arbitrary", "arbitrary"),
        name="adaln_mod",
    )(cvec, w_ada, b_ada.reshape(DEPTH, 1, n))


def _x_specs(x, tm, tn, index):
    if not isinstance(x, tuple):
        return (x,), (pl.BlockSpec((tm, tn), index),)
    n_ctx = N_CTX_TOK // tm

    def ctx_index(*grid):
        r, c = index(*grid)
        return jnp.minimum(r, n_ctx - 1), c

    def lat_index(*grid):
        r, c = index(*grid)
        return jnp.maximum(r - n_ctx, 0), c
    return x, (pl.BlockSpec((tm, tn), ctx_index), pl.BlockSpec((tm, tn), lat_index))


def _x_tile(x_refs, i, tm):
    if len(x_refs) == 1:
        return x_refs[0][...]
    return jnp.where(i < N_CTX_TOK // tm, x_refs[0][...], x_refs[1][...])


def _prenorm_kernel(*refs, tm, n_x):
    nw_ref, sh_ref, sc_ref, o_ref = refs[n_x:]
    i = pl.program_id(0)
    g = _group_of_tile(i, tm)
    x = _x_tile(refs[:n_x], i, tm)
    ms = jnp.mean(x * x, axis=-1, keepdims=True)
    y = x * lax.rsqrt(ms + EPS) * nw_ref[...]
    o_ref[...] = (y * (1.0 + sc_ref[pl.ds(g, 1), :]) + sh_ref[pl.ds(g, 1), :]).astype(BF16)


def _prenorm_call(x, norm_w, mod, l, which):
    tm = 1024
    xs, x_specs = _x_specs(x, tm, D_MODEL, lambda i: (i, 0))
    return pl.pallas_call(
        functools.partial(_prenorm_kernel, tm=tm, n_x=len(xs)),
        grid=(N_TOK // tm,),
        in_specs=[
            *x_specs,
            pl.BlockSpec((None, 1, D_MODEL), lambda i: (2 * l + which, 0, 0)),
            pl.BlockSpec((None, MOD_ROWS, D_MODEL), lambda i: (l, 0, 3 * which)),
            pl.BlockSpec((None, MOD_ROWS, D_MODEL), lambda i: (l, 0, 3 * which + 1)),
        ],
        out_specs=pl.BlockSpec((tm, D_MODEL), lambda i: (i, 0)),
        out_shape=jax.ShapeDtypeStruct((N_TOK, D_MODEL), BF16),
        compiler_params=_params("arbitrary"),
        name=f"prenorm{which}",
    )(*xs, norm_w.reshape(DEPTH * 2, 1, D_MODEL), mod, mod)


def _w_spec(k, tn, l, col_off):
    if col_off % tn == 0:
        return pl.BlockSpec((None, k, tn), lambda j, i: (l, 0, col_off // tn + j))
    return pl.BlockSpec((pl.Element(1), pl.Element(k), pl.Element(tn)),
                        lambda j, i: (l, 0, pl.multiple_of(col_off + j * tn, LANES)))


def _cast_weight(w_ref, wb_ref):
    w = w_ref[0] if len(w_ref.shape) == 3 else w_ref[...]
    wb_ref[...] = w.astype(BF16)


def _mm_kernel(*refs, n_h, n_extra, epilogue, tm):
    w_ref = refs[n_h]
    extra = refs[n_h + 1:n_h + 1 + n_extra]
    o_ref, wb_ref = refs[n_h + 1 + n_extra:]
    j, i = pl.program_id(0), pl.program_id(1)

    @pl.when(i == 0)
    def _():
        _cast_weight(w_ref, wb_ref)

    acc = jnp.dot(_x_tile(refs[:n_h], i, tm), wb_ref[...], preferred_element_type=F32)
    epilogue(acc, o_ref, j, i, *extra)


def _store(acc, o_ref, j, i):
    o_ref[...] = acc.astype(o_ref.dtype)


def _ws_matmul(h, w, l, col_off, n_cols, tm, tn, out_dtype, name, extra=(), extra_specs=(), epilogue=_store):
    k = h[0].shape[1] if isinstance(h, tuple) else h.shape[1]
    hs, h_specs = _x_specs(h, tm, k, lambda j, i: (i, 0))
    return pl.pallas_call(
        functools.partial(_mm_kernel, n_h=len(hs), n_extra=len(extra), epilogue=epilogue, tm=tm),
        grid=(n_cols // tn, N_TOK // tm),
        in_specs=[*h_specs, _w_spec(k, tn, l, col_off), *extra_specs],
        out_specs=pl.BlockSpec((tm, tn), lambda j, i: (i, j)),
        out_shape=jax.ShapeDtypeStruct((N_TOK, n_cols), out_dtype),
        scratch_shapes=[pltpu.VMEM((k, tn), BF16)],
        compiler_params=_params("arbitrary", "arbitrary"),
        name=name,
    )(*hs, w, *extra)


def _silu_store(acc, o_ref, j, i):
    o_ref[...] = _silu(acc)


def _sigmoid_store(acc, o_ref, j, i):
    o_ref[...] = _sigmoid(acc)


def _conv_silu_store(acc, o_ref, j, i, cw_ref, cb_ref, *, tm):
    last = jnp.where(i < N_CTX_TOK // tm, SEQ - 1, GRID_W - 1)
    o_ref[...] = _silu(_row_conv3(acc, cw_ref[...], last) + cb_ref[...])


def _scu_kernel(h_ref, wb_ref, wc_ref, wh_ref, cw_ref, o_ref, wb_s, wc_s, wh_s, *, tm):
    i = pl.program_id(1)

    @pl.when(i == 0)
    def _():
        _cast_weight(wb_ref, wb_s)
        _cast_weight(wc_ref, wc_s)
        _cast_weight(wh_ref, wh_s)

    last = jnp.where(i < N_CTX_TOK // tm, SEQ - 1, GRID_W - 1)
    h = h_ref[...]
    v = jnp.dot(h, wc_s[...], preferred_element_type=F32) * jnp.dot(h, wh_s[...], preferred_element_type=F32)
    o_ref[...] = (jnp.dot(h, wb_s[...], preferred_element_type=F32) * _row_conv3(v, cw_ref[...], last)).astype(BF16)


def _scu_call(h, w_in, sc_conv_w, l):
    tm, tn = 1024, 512
    return pl.pallas_call(
        functools.partial(_scu_kernel, tm=tm),
        grid=(D_MODEL // tn, N_TOK // tm),
        in_specs=[
            pl.BlockSpec((tm, D_MODEL), lambda j, i: (i, 0)),
            _w_spec(D_MODEL, tn, l, COL_SC),
            _w_spec(D_MODEL, tn, l, COL_SC + D_MODEL),
            _w_spec(D_MODEL, tn, l, COL_SC + 2 * D_MODEL),
            pl.BlockSpec((None, 3, tn), lambda j, i: (l, 0, j)),
        ],
        out_specs=pl.BlockSpec((tm, tn), lambda j, i: (i, j)),
        out_shape=jax.ShapeDtypeStruct((N_TOK, D_MODEL), BF16),
        scratch_shapes=[pltpu.VMEM((D_MODEL, tn), BF16)] * 3,
        compiler_params=_params("arbitrary", "arbitrary"),
        name="inproj_scu",
    )(h, w_in, w_in, w_in, sc_conv_w)


def _branch_mix_kernel(yn_ref, u_ref, wa_ref, wb_ref, ga_ref, gb_ref, o_ref, wa_s, wb_s):
    @pl.when(pl.program_id(1) == 0)
    def _():
        _cast_weight(wa_ref, wa_s)
        _cast_weight(wb_ref, wb_s)

    y_ssm = jnp.dot(yn_ref[...], wa_s[...], preferred_element_type=F32)
    y_sc = jnp.dot(u_ref[...], wb_s[...], preferred_element_type=F32)
    o_ref[...] = (ga_ref[...] * y_ssm + gb_ref[...] * y_sc).astype(BF16)


def _branch_mix_call(yn, u_sc, w_out_ssm, w_out_sc, sg, l):
    tm, tn = 512, 512
    nb = D_MODEL // tn
    return pl.pallas_call(
        _branch_mix_kernel,
        grid=(nb, N_TOK // tm),
        in_specs=[
            pl.BlockSpec((tm, D_INNER), lambda j, i: (i, 0)),
            pl.BlockSpec((tm, D_MODEL), lambda j, i: (i, 0)),
            _w_spec(D_INNER, tn, l, 0),
            _w_spec(D_MODEL, tn, l, 0),
            pl.BlockSpec((tm, tn), lambda j, i: (i, j)),
            pl.BlockSpec((tm, tn), lambda j, i: (i, nb + j)),
        ],
        out_specs=pl.BlockSpec((tm, tn), lambda j, i: (i, j)),
        out_shape=jax.ShapeDtypeStruct((N_TOK, D_MODEL), BF16),
        scratch_shapes=[pltpu.VMEM((D_INNER, tn), BF16), pltpu.VMEM((D_MODEL, tn), BF16)],
        compiler_params=_params("arbitrary", "arbitrary"),
        name="branch_mix",
    )(yn, u_sc, w_out_ssm, w_out_sc, sg, sg)


def _merge_norm_kernel(*refs, tm, n_x):
    m_ref, w_ref = refs[:2]
    x_refs = refs[2:2 + n_x]
    mod_ref, nw_ref, x1_ref, h_ref, wb_s = refs[2 + n_x:]
    i = pl.program_id(0)

    @pl.when(i == 0)
    def _():
        _cast_weight(w_ref, wb_s)

    mod = mod_ref[pl.ds(_group_of_tile(i, tm), 1), :]
    part = lambda k: mod[:, k * D_MODEL:(k + 1) * D_MODEL]
    acc = jnp.dot(m_ref[...], wb_s[...], preferred_element_type=F32)
    x1 = _x_tile(x_refs, i, tm) + part(2) * acc
    x1_ref[...] = x1
    ms = jnp.mean(x1 * x1, axis=-1, keepdims=True)
    y = x1 * lax.rsqrt(ms + EPS) * nw_ref[...]
    h_ref[...] = (y * (1.0 + part(4)) + part(3)).astype(BF16)


def _merge_norm_call(m, w_o, x, mod, norm_w, l):
    tm = 256 if isinstance(x, tuple) else 512
    xs, x_specs = _x_specs(x, tm, D_MODEL, lambda i: (i, 0))
    row_tile = pl.BlockSpec((tm, D_MODEL), lambda i: (i, 0))
    return pl.pallas_call(
        functools.partial(_merge_norm_kernel, tm=tm, n_x=len(xs)),
        grid=(N_TOK // tm,),
        in_specs=[
            row_tile,
            pl.BlockSpec((None, D_MODEL, D_MODEL), lambda i: (l, 0, 0), pipeline_mode=pl.Buffered(1)),
            *x_specs,
            pl.BlockSpec((None, MOD_ROWS, 6 * D_MODEL), lambda i: (l, 0, 0)),
            pl.BlockSpec((None, 1, D_MODEL), lambda i: (2 * l + 1, 0, 0)),
        ],
        out_specs=[row_tile, row_tile],
        out_shape=[jax.ShapeDtypeStruct((N_TOK, D_MODEL), F32), jax.ShapeDtypeStruct((N_TOK, D_MODEL), BF16)],
        scratch_shapes=[pltpu.VMEM((D_MODEL, D_MODEL), BF16)],
        compiler_params=_params("arbitrary"),
        name="merge_norm",
    )(m, w_o, *xs, mod, norm_w.reshape(DEPTH * 2, 1, D_MODEL))


def _ffn_up_kernel(h_ref, wg_ref, wv_ref, cg_ref, cv_ref, o_ref, wg_s, wv_s, g_s, v_s, *, row_len, vertical):
    tm = h_ref.shape[0]
    halo = row_len if vertical else 0

    @pl.when(pl.program_id(1) == 0)
    def _():
        _cast_weight(wg_ref, wg_s)
        _cast_weight(wv_ref, wv_s)
        if halo:
            for s in (g_s, v_s):
                s[0:halo, :] = jnp.zeros((halo, s.shape[1]), F32)
                s[halo + tm:, :] = jnp.zeros((halo, s.shape[1]), F32)

    def conv(s_ref, w, r0):
        ext = s_ref[r0:r0 + FFN_ROWS + 2 * halo, :]
        n = ext.shape[0]
        pos = lax.broadcasted_iota(jnp.int32, ext.shape, 0) & (row_len - 1)
        xl = jnp.where(pos == 0, 0.0, pltpu.roll(ext, 1, 0))
        xr = jnp.where(pos == row_len - 1, 0.0, pltpu.roll(ext, n - 1, 0))
        y = None
        for a in ((0, 1, 2) if vertical else (1,)):
            sl = slice(halo + (a - 1) * row_len, halo + (a - 1) * row_len + FFN_ROWS) if vertical else slice(0, n)
            term = xl[sl] * w[3 * a:3 * a + 1] + ext[sl] * w[3 * a + 1:3 * a + 2] + xr[sl] * w[3 * a + 2:3 * a + 3]
            y = term if y is None else y + term
        return y

    cg, cv = cg_ref[...], cv_ref[...]
    n_blocks = tm // FFN_ROWS
    for b in range(n_blocks + 1):
        if b < n_blocks:
            hb = h_ref[b * FFN_ROWS:(b + 1) * FFN_ROWS, :]
            rows = slice(halo + b * FFN_ROWS, halo + (b + 1) * FFN_ROWS)
            g_s[rows, :] = jnp.dot(hb, wg_s[...], preferred_element_type=F32)
            v_s[rows, :] = jnp.dot(hb, wv_s[...], preferred_element_type=F32)
        if b >= 1:
            r0 = (b - 1) * FFN_ROWS
            o_ref[r0:r0 + FFN_ROWS, :] = (_silu(conv(g_s, cg, r0)) * conv(v_s, cv, r0)).astype(BF16)


def _ffn_up_call(h, w_up, ffn_conv_w, l, latent):
    tm, tn = DEC_SEQ, 512
    nb = D_FF // tn
    w9 = ffn_conv_w.reshape(DEPTH, 9, 2 * D_FF)
    tile0, n_rows = (N_CTX_TOK // tm, N_LAT_TOK) if latent else (0, N_CTX_TOK)
    return pl.pallas_call(
        functools.partial(_ffn_up_kernel, row_len=GRID_W if latent else SEQ, vertical=latent),
        grid=(nb, n_rows // tm),
        in_specs=[
            pl.BlockSpec((tm, D_MODEL), lambda j, i: (tile0 + i, 0)),
            _w_spec(D_MODEL, tn, l, 0),
            _w_spec(D_MODEL, tn, l, D_FF),
            pl.BlockSpec((None, 9, tn), lambda j, i: (l, 0, j)),
            pl.BlockSpec((None, 9, tn), lambda j, i: (l, 0, nb + j)),
        ],
        out_specs=pl.BlockSpec((tm, tn), lambda j, i: (i, j)),
        out_shape=jax.ShapeDtypeStruct((n_rows, D_FF), BF16),
        scratch_shapes=[pltpu.VMEM((D_MODEL, tn), BF16)] * 2
        + [pltpu.VMEM((tm + (2 * GRID_W if latent else 0), tn), F32)] * 2,
        compiler_params=_params("arbitrary", "arbitrary"),
        name="ffn_up_lat" if latent else "ffn_up_ctx",
    )(h, w_up, w_up, w9, w9)


def _prefix_sum_rows(a, qi):
    s = 1
    while s < CHUNK:
        a = a + jnp.where(qi >= s, pltpu.roll(a, s, 0), 0.0)
        s *= 2
    return a


def _lane_bcast(a, j):
    return jnp.broadcast_to(a[:, j:j + 1], (a.shape[0], LANES))


def _ssd_kernel(*refs, aliased, layer):
    (zs_ref, xs_ref, b_ref, c_ref, dt_ref, dtb_ref, al_ref, dsk_ref, nw_ref, h0_ref) = refs[:10]
    (y_ref, st_ref, cum_s, rt_s, wt_s, tot_s, bt_s, cm_s, cb_s, rhs_s,
     y_s, stf_s, stb_s) = refs[10 + int(aliased):]
    sb = pl.program_id(1)
    dt_shift = lax.rem(LANES - 2 * HEADS_PER_GROUP * pl.program_id(0), LANES)
    a2_row = -jnp.exp(al_ref[...]) * LOG2E
    qi = lax.broadcasted_iota(jnp.int32, (CHUNK, CHUNK), 0)
    ki = lax.broadcasted_iota(jnp.int32, (CHUNK, CHUNK), 1)
    lo = ki < HEAD_DIM
    lo_row = lo[0:1]
    n_lanes_used = 2 * HEADS_PER_GROUP
    fwd_lane = ki < HEADS_PER_GROUP
    fwd_row = lax.broadcasted_iota(jnp.int32, (n_lanes_used, 1), 0) < HEADS_PER_GROUP

    def prep(p, carry):
        rows = pl.ds(pl.multiple_of(p * CHUNK, CHUNK), CHUNK)
        dtv = _softplus(pltpu.roll(dt_ref[rows, :], dt_shift, 1) + dtb_ref[...])
        a = dtv * a2_row
        pre = _prefix_sum_rows(a, qi)
        tot = pre[CHUNK - 1:CHUNK, :]
        cum = jnp.where(fwd_lane, pre, tot - pre + a)
        cum_t = cum.T[0:n_lanes_used]
        dt_t = dtv.T[0:n_lanes_used]
        tot_t = jnp.where(fwd_row, cum_t[:, CHUNK - 1:CHUNK], cum_t[:, 0:1])
        cum_s[p] = cum
        tot_s[p] = jnp.broadcast_to(tot, tot_s.shape[1:])
        rt_s[p] = cum_t - jnp.log2(dt_t)
        wt_s[p] = jnp.exp2(tot_t - cum_t) * dt_t
        bt = b_ref[rows, :].T
        cm = c_ref[rows, :].astype(BF16)
        bt_s[p] = bt
        cm_s[rows, :] = cm
        cb_s[p] = jnp.dot(cm, bt.astype(BF16), preferred_element_type=F32)
        for t in range(GROUP_DIM // LANES):
            xt = xs_ref[rows, t * LANES:(t + 1) * LANES]
            rhs_s[p, t] = jnp.concatenate([jnp.where(lo, xt, 0.0), jnp.where(lo, 0.0, xt)], axis=0).astype(BF16)
        return carry
    lax.fori_loop(0, SSD_BLOCK // CHUNK, prep, 0, unroll=8)

    def chunk_head(c, d, st_s, zero_state):
        rows = pl.ds(pl.multiple_of(c * CHUNK, CHUNK), CHUNK)
        p = dict(c=c, rows=rows, cum=cum_s[c], tot=tot_s[c, 0:1], rt=rt_s[c], wt=wt_s[c], bt=bt_s[c],
                 cb=cb_s[c], tri=(qi >= ki) if d == 0 else (qi <= ki))
        if not zero_state:
            p["s_in"] = st_s[...]
            p["y_off"] = jnp.dot(cm_s[rows, :], p["s_in"].astype(BF16), preferred_element_type=F32)
        return p

    def head_pair(t, d, p, st_s, first_visit):
        tile = slice(t * LANES, (t + 1) * LANES)
        ha, hb = HEADS_PER_GROUP * d + 2 * t, HEADS_PER_GROUP * d + 2 * t + 1
        cols, tops, bots = [], [], []
        for hl in (ha, hb):
            col = _lane_bcast(p["cum"], hl)
            decay_dt = jnp.exp2(jnp.where(p["tri"], col - p["rt"][hl:hl + 1, :], -jnp.inf))
            cols.append(col)
            tops.append((p["cb"] * decay_dt).astype(BF16))
            bots.append((p["bt"] * p["wt"][hl:hl + 1, :]).astype(BF16))
        lhs = jnp.concatenate([jnp.concatenate(tops, axis=1), jnp.concatenate(bots, axis=1)], axis=0)
        out = jnp.dot(lhs, rhs_s[p["c"], t], preferred_element_type=F32)
        y, s_new = out[0:CHUNK], out[CHUNK:2 * CHUNK]
        if "s_in" in p:
            e_in = jnp.exp2(jnp.where(lo, cols[0], cols[1]))
            y = y + p["y_off"][:, tile] * e_in
            dec = jnp.exp2(jnp.where(lo_row, _lane_bcast(p["tot"], ha), _lane_bcast(p["tot"], hb)))
            s_new = s_new + p["s_in"][:, tile] * dec
        st_s[:, tile] = s_new
        y_s[p["rows"], tile] = y if first_visit else y_s[p["rows"], tile] + y

    def finish_rows(rows):
        y = y_s[rows, :] + dsk_ref[...] * xs_ref[rows, :]
        yz = y * zs_ref[rows, :]
        msq = jnp.mean(yz * yz, axis=-1, keepdims=True)
        y_ref[rows, :] = (yz * lax.rsqrt(msq + EPS) * nw_ref[...]).astype(BF16)

    def chunk_both(c_fwd, c_bwd, first_visit, zero_state=False):
        pf = chunk_head(c_fwd, 0, stf_s, zero_state)
        pb = chunk_head(c_bwd, 1, stb_s, zero_state)
        for t in range(GROUP_DIM // LANES):
            head_pair(t, 0, pf, stf_s, first_visit)
            head_pair(t, 1, pb, stb_s, first_visit)
        if not first_visit:
            finish_rows(pf["rows"])
            finish_rows(pb["rows"])

    @pl.when(sb < N_CTX_TOK // SSD_BLOCK)
    def _context():
        nc = SEQ // CHUNK

        def seq_body(s, carry):
            chunk_both(s * nc, s * nc + 1, first_visit=True, zero_state=True)
            chunk_both(s * nc + 1, s * nc, first_visit=False)
            for d, st_s in enumerate((stf_s, stb_s)):
                if aliased:
                    st_ref[s, d] = st_s[...].T
                else:
                    for other in range(DEPTH):
                        st_ref[s, other, d] = st_s[...].T if other == layer else jnp.zeros(st_ref.shape[3:], F32)
            return carry
        lax.fori_loop(0, SSD_BLOCK // SEQ, seq_body, 0, unroll=2)

    @pl.when(sb >= N_CTX_TOK // SSD_BLOCK)
    def _latent():
        nc = DEC_SEQ // CHUNK
        stf_s[...] = h0_ref[0].T
        stb_s[...] = h0_ref[1].T

        def first_half(k, carry):
            chunk_both(k, nc - 1 - k, first_visit=True)
            return carry
        lax.fori_loop(0, nc // 2, first_half, 0, unroll=4)

        def second_half(k, carry):
            chunk_both(k, nc - 1 - k, first_visit=False)
            return carry
        lax.fori_loop(nc // 2, nc, second_half, 0, unroll=4)


def _ssd_call(zs, xbc, dt_raw, dtb_x, alog_x, dsk, ssm_norm_w, state_ssm, prev_states, l):
    n_ctx_blk = N_CTX_TOK // SSD_BLOCK
    n_blk = N_TOK // SSD_BLOCK
    seq_per_blk = SSD_BLOCK // SEQ
    n_chunks = SSD_BLOCK // CHUNK
    b_cb = D_INNER // D_STATE
    c_cb = b_cb + N_GROUPS
    rowvec = lambda width: pl.BlockSpec((None, 1, width), lambda g, sb: (l, 0, g))
    h0 = state_ssm.reshape(DEC_BATCH, DEPTH, 2, N_GROUPS, GROUP_DIM, D_STATE)
    aliased = prev_states is not None
    inputs = [zs, xbc, xbc, xbc, dt_raw, dtb_x, alog_x, dsk, ssm_norm_w, h0]
    in_specs = [
        pl.BlockSpec((SSD_BLOCK, GROUP_DIM), lambda g, sb: (sb, g)),
        pl.BlockSpec((SSD_BLOCK, GROUP_DIM), lambda g, sb: (sb, g)),
        pl.BlockSpec((SSD_BLOCK, D_STATE), lambda g, sb: (sb, b_cb + g)),
        pl.BlockSpec((SSD_BLOCK, D_STATE), lambda g, sb: (sb, c_cb + g)),
        pl.BlockSpec((SSD_BLOCK, LANES), lambda g, sb: (sb, 0)),
        rowvec(LANES), rowvec(LANES), rowvec(GROUP_DIM), rowvec(GROUP_DIM),
        pl.BlockSpec((None, None, 2, None, GROUP_DIM, D_STATE),
                     lambda g, sb: (jnp.maximum(sb - n_ctx_blk, 0), l, 0, g, 0, 0)),
    ]
    if aliased:
        inputs.append(prev_states)
        in_specs.append(pl.BlockSpec(memory_space=pl.ANY))
    return pl.pallas_call(
        functools.partial(_ssd_kernel, aliased=aliased, layer=l),
        grid=(N_GROUPS, n_blk),
        in_specs=in_specs,
        out_specs=[
            pl.BlockSpec((SSD_BLOCK, GROUP_DIM), lambda g, sb: (sb, g)),
            pl.BlockSpec((seq_per_blk, None if aliased else DEPTH, 2, None, GROUP_DIM, D_STATE),
                         lambda g, sb: (jnp.minimum(sb, n_ctx_blk - 1), l if aliased else 0, 0, g, 0, 0)),
        ],
        out_shape=[
            jax.ShapeDtypeStruct((N_TOK, D_INNER), BF16),
            jax.ShapeDtypeStruct((BATCH, DEPTH, 2, N_GROUPS, GROUP_DIM, D_STATE), F32),
        ],
        scratch_shapes=[
            pltpu.VMEM((n_chunks, CHUNK, LANES), F32),
            pltpu.VMEM((n_chunks, 2 * HEADS_PER_GROUP, CHUNK), F32),
            pltpu.VMEM((n_chunks, 2 * HEADS_PER_GROUP, CHUNK), F32),
            pltpu.VMEM((n_chunks, 8, LANES), F32),
            pltpu.VMEM((n_chunks, D_STATE, CHUNK), F32),
            pltpu.VMEM((SSD_BLOCK, D_STATE), BF16),
            pltpu.VMEM((n_chunks, CHUNK, CHUNK), F32),
            pltpu.VMEM((n_chunks, GROUP_DIM // LANES, 2 * CHUNK, LANES), BF16),
            pltpu.VMEM((SSD_BLOCK, GROUP_DIM), F32),
            pltpu.VMEM((D_STATE, GROUP_DIM), F32),
            pltpu.VMEM((D_STATE, GROUP_DIM), F32),
        ],
        input_output_aliases={len(inputs) - 1: 1} if aliased else {},
        compiler_params=_params("arbitrary", "arbitrary"),
        name="ssd",
    )(*inputs)


def _final_norm_kernel(x_ref, w_ref, o_ref):
    x = x_ref[...]
    ms = jnp.mean(x * x, axis=-1, keepdims=True)
    o_ref[...] = x * lax.rsqrt(ms + EPS) * w_ref[...]


def _final_norm_call(x_all, w, row0, n_rows):
    tm = 1024
    return pl.pallas_call(
        _final_norm_kernel,
        grid=(n_rows // tm,),
        in_specs=[
            pl.BlockSpec((tm, D_MODEL), lambda i: (row0 // tm + i, 0)),
            pl.BlockSpec((1, D_MODEL), lambda i: (0, 0)),
        ],
        out_specs=pl.BlockSpec((tm, D_MODEL), lambda i: (i, 0)),
        out_shape=jax.ShapeDtypeStruct((n_rows, D_MODEL), F32),
        compiler_params=_params("arbitrary"),
        name="final_norm",
    )(x_all, w.reshape(1, D_MODEL))


def _per_group_lanes(a, pad):
    lead = a.shape[:-2]
    a = a.reshape(*lead, 2, N_GROUPS, HEADS_PER_GROUP)
    a = jnp.moveaxis(a, -3, -2).reshape(*lead, N_GROUPS, 2 * HEADS_PER_GROUP)
    if pad:
        a = jnp.pad(a, [(0, 0)] * (len(lead) + 1) + [(0, LANES - 2 * HEADS_PER_GROUP)])
    return a.reshape(*lead, -1)


def kernel(x_prompt, x_sample, state_ssm, c, c_ctx, w_ada, b_ada, norm_w, w_in, conv_w, conv_b, a_log,
           dt_bias, d_skip, ssm_norm_w, w_out_ssm, sc_conv_w, w_out_sc, w_o, w_up, ffn_conv_w, w_down,
           final_norm_w):
    x_all = (x_prompt.reshape(N_CTX_TOK, D_MODEL), x_sample.reshape(N_LAT_TOK, D_MODEL))
    cvec = jnp.zeros((MOD_ROWS, D_MODEL), F32).at[0].set(c_ctx).at[1:1 + DEC_BATCH].set(c)
    mod = _mod_call(cvec, w_ada, b_ada)

    w_dt = _per_group_lanes(w_in[:, :, COL_DT:COL_SC].reshape(DEPTH, D_MODEL, 2, N_HEADS), pad=False)
    dtb_x = _per_group_lanes(dt_bias, pad=True).reshape(DEPTH, 1, N_GROUPS * LANES)
    alog_x = _per_group_lanes(a_log, pad=True).reshape(DEPTH, 1, N_GROUPS * LANES)
    dsk = jnp.repeat(d_skip, HEAD_DIM, axis=-1).reshape(DEPTH, 1, D_INNER)
    conv_b3 = conv_b.reshape(DEPTH, 1, CONV_DIM)
    ssm_nw3 = ssm_norm_w.reshape(DEPTH, 1, D_INNER)

    tm = 1024
    states = None
    for l in range(DEPTH):
        def gate_of_mod(which, tn):
            return pl.BlockSpec((None, MOD_ROWS, tn), lambda j, i: (l, 0, (2 + 3 * which) * (D_MODEL // tn) + j))

        def residual(tile_m):
            def epilogue(acc, o_ref, j, i, *refs):
                g_ref = refs[-1]
                o_ref[...] = (_x_tile(refs[:-1], i, tile_m)
                              + g_ref[pl.ds(_group_of_tile(i, tile_m), 1), :] * acc)
            return epilogue

        h = _prenorm_call(x_all, norm_w, mod, l, 0)
        zs = _ws_matmul(h, w_in, l, 0, D_INNER, tm, 1024, F32, "inproj_z", epilogue=_silu_store)
        xbc = _ws_matmul(
            h, w_in, l, D_INNER, CONV_DIM, tm, 1024, F32, "inproj_xbc",
            extra=(conv_w, conv_b3),
            extra_specs=(pl.BlockSpec((None, 3, 1024), lambda j, i: (l, 0, j)),
                         pl.BlockSpec((None, 1, 1024), lambda j, i: (l, 0, j))),
            epilogue=functools.partial(_conv_silu_store, tm=tm))
        dt_raw = _ws_matmul(h, w_dt, l, 0, LANES, 2 * tm, LANES, F32, "inproj_dt")
        u_sc = _scu_call(h, w_in, sc_conv_w, l)
        sg = _ws_matmul(h, w_in, l, COL_GATES, 2 * D_MODEL, tm, 1024, F32, "inproj_gates", epilogue=_sigmoid_store)

        yn, states = _ssd_call(zs, xbc, dt_raw, dtb_x, alog_x, dsk, ssm_nw3, state_ssm, states, l)

        m = _branch_mix_call(yn, u_sc, w_out_ssm, w_out_sc, sg, l)
        x_all, h2 = _merge_norm_call(m, w_o, x_all, mod, norm_w, l)
        act = (_ffn_up_call(h2, w_up, ffn_conv_w, l, False), _ffn_up_call(h2, w_up, ffn_conv_w, l, True))
        tmd = 512
        x_all = _ws_matmul(
            act, w_down, l, 0, D_MODEL, tmd, 512, F32, "ffn_down",
            extra=(x_all, mod),
            extra_specs=(pl.BlockSpec((tmd, 512), lambda j, i: (i, j)), gate_of_mod(1, 512)),
            epilogue=residual(tmd))

    y_prompt = _final_norm_call(x_all, final_norm_w, 0, N_CTX_TOK).reshape(BATCH, SEQ, D_MODEL)
    y_sample = _final_norm_call(x_all, final_norm_w, N_CTX_TOK, N_LAT_TOK).reshape(DEC_BATCH, DEC_SEQ, D_MODEL)
    new_state = states.reshape(BATCH, DEPTH, 2, N_HEADS, HEAD_DIM, D_STATE)
    return (y_prompt, y_sample, new_state)
```
